```python
import jax, jax.numpy as jnp
from jax import lax
import numpy as np

D_MODEL = 1024
BATCH = 32
SEQ = 2048
DEPTH = 1

D_MIX = D_MODEL
SG_WIDTH = D_MIX // 2
SG_GROUPS = 8
SG_GROUP_DIM = SG_WIDTH // SG_GROUPS
SG_CHUNK = 128
DN_WIDTH = D_MIX - SG_WIDTH
DN_HEAD_DIM = 128
DN_HEADS = DN_WIDTH // DN_HEAD_DIM
DN_CHUNK = 64
CONV_K = 4
D_FF = 2816
EPS = 1e-6
IN_COLS = 2 * SG_WIDTH + 4 * DN_WIDTH + 2 * DN_HEADS

kernel_name = "hybrid_gmlp_gated_deltanet_macaron"


def rmsnorm(x, g):
    xf = x.astype(jnp.float32)
    y = xf * lax.rsqrt(jnp.mean(xf * xf, axis=-1, keepdims=True) + EPS)
    return (y * g.astype(jnp.float32)).astype(x.dtype)


def layernorm(x, g, b):
    xf = x.astype(jnp.float32)
    mu = jnp.mean(xf, axis=-1, keepdims=True)
    var = jnp.mean(jnp.square(xf - mu), axis=-1, keepdims=True)
    y = (xf - mu) * lax.rsqrt(var + EPS)
    return (y * g.astype(jnp.float32) + b.astype(jnp.float32)).astype(x.dtype)


def l2norm(x):
    return x * lax.rsqrt(jnp.sum(x * x, axis=-1, keepdims=True) + EPS)


def swiglu(h, w_gate, w_up, w_down):
    return (jax.nn.silu(h @ w_gate) * (h @ w_up)) @ w_down


def causal_dwconv(x, w):
    k_taps = w.shape[0]
    t_len = x.shape[1]
    xp = jnp.pad(x, ((0, 0), (k_taps - 1, 0), (0, 0)))
    y = xp[:, 0:t_len] * w[0]
    for j in range(1, k_taps):
        y = y + xp[:, j:j + t_len] * w[j]
    return y


def chunked_spatial_gating(u, v, ln_g, ln_b, w_s, b_s):
    bsz, t_len, _ = v.shape
    n_chunks = t_len // SG_CHUNK
    v = layernorm(v, ln_g, ln_b).reshape(bsz, n_chunks, SG_CHUNK, SG_GROUPS, SG_GROUP_DIM)
    pos = jnp.arange(SG_CHUNK)
    causal = pos[:, None] >= pos[None, :]
    w_causal = jnp.where(causal, w_s, jnp.zeros((), w_s.dtype))
    vs = jnp.einsum('gts,bnsgc->bntgc', w_causal, v) + b_s.T[:, :, None]
    return u * vs.reshape(bsz, t_len, SG_WIDTH)


def gated_delta_rule(q, k, v, g, beta):
    bsz, t_len, n_heads, dk = q.shape
    dv = v.shape[-1]
    c = DN_CHUNK
    n_chunks = t_len // c

    def chunks(a):
        return a.reshape(bsz, n_chunks, c, n_heads, a.shape[-1]).transpose(1, 0, 3, 2, 4)

    q = chunks(q) * (dk ** -0.5)
    k = chunks(k)
    v = chunks(v)
    g = chunks(g[..., None])[..., 0]
    beta = chunks(beta[..., None])[..., 0]
    gc = jnp.cumsum(g, axis=-1)
    pos = jnp.arange(c)
    incl = pos[:, None] >= pos[None, :]
    strict = pos[:, None] > pos[None, :]
    decay = jnp.exp(jnp.where(incl, gc[..., :, None] - gc[..., None, :], -jnp.inf))
    k_beta = k * beta[..., None]
    v_beta = v * beta[..., None]
    l_mat = jnp.where(strict, jnp.einsum('nbhcd,nbhsd->nbhcs', k_beta, k) * decay, 0.0)
    rhs = jnp.concatenate([v_beta, k_beta * jnp.exp(gc)[..., None]], axis=-1)
    sol = lax.linalg.triangular_solve(l_mat, rhs, left_side=True, lower=True, unit_diagonal=True)
    u_wy, w_wy = sol[..., :dv], sol[..., dv:]
    qk = jnp.einsum('nbhcd,nbhsd->nbhcs', q, k) * decay
    q_dec = q * jnp.exp(gc)[..., None]
    k_dec = k * jnp.exp(gc[..., -1:] - gc)[..., None]
    g_last = jnp.exp(gc[..., -1])

    def step(state, xs):
        q_n, k_n, u_n, w_n, qk_n, gl_n = xs
        v_new = u_n - jnp.einsum('bhcd,bhde->bhce', w_n, state)
        o_n = jnp.einsum('bhcd,bhde->bhce', q_n, state) + jnp.einsum('bhcs,bhse->bhce', qk_n, v_new)
        state = state * gl_n[..., None, None] + jnp.einsum('bhcd,bhce->bhde', k_n, v_new)
        return state, o_n

    s0 = jnp.zeros((bsz, n_heads, dk, dv), jnp.float32)
    _, o = lax.scan(step, s0, (q_dec, k_dec, u_wy, w_wy, qk, g_last))
    return o.transpose(1, 0, 3, 2, 4).reshape(bsz, t_len, n_heads, dv)


def hybrid_mixer(h, w_in, conv_w, a_log, dt_bias, dn_norm, sg_ln_g, sg_ln_b, sg_w, sg_b, w_out):
    bsz, t_len, _ = h.shape
    proj = h @ w_in
    o1 = SG_WIDTH
    o2 = 2 * SG_WIDTH
    o3 = o2 + 3 * DN_WIDTH
    o4 = o3 + DN_WIDTH
    o5 = o4 + DN_HEADS
    sg_u, sg_v, dn_qkv, dn_z, dn_b, dn_a = jnp.split(proj, [o1, o2, o3, o4, o5], axis=-1)

    sg_out = chunked_spatial_gating(jax.nn.gelu(sg_u), jax.nn.gelu(sg_v), sg_ln_g, sg_ln_b, sg_w, sg_b)

    qkv = jax.nn.silu(causal_dwconv(dn_qkv, conv_w)).astype(jnp.float32)
    q, k, v = jnp.split(qkv, 3, axis=-1)
    q = l2norm(q.reshape(bsz, t_len, DN_HEADS, DN_HEAD_DIM))
    k = l2norm(k.reshape(bsz, t_len, DN_HEADS, DN_HEAD_DIM))
    v = v.reshape(bsz, t_len, DN_HEADS, DN_HEAD_DIM)
    beta = jax.nn.sigmoid(dn_b.astype(jnp.float32))
    g = -jnp.exp(a_log.astype(jnp.float32)) * jax.nn.softplus(dn_a.astype(jnp.float32) + dt_bias.astype(jnp.float32))
    o = gated_delta_rule(q, k, v, g, beta)
    o = o * lax.rsqrt(jnp.mean(o * o, axis=-1, keepdims=True) + EPS) * dn_norm.astype(jnp.float32)
    z = dn_z.reshape(bsz, t_len, DN_HEADS, DN_HEAD_DIM).astype(jnp.float32)
    dn_out = (o * jax.nn.silu(z)).reshape(bsz, t_len, DN_WIDTH).astype(h.dtype)

    return jnp.concatenate([sg_out, dn_out], axis=-1) @ w_out


def setup_inputs(seed: int = 0) -> dict:
    key = jax.random.key(seed)
    ks = jax.random.split(key, 24)
    f32 = jnp.float32

    def nrm(k, shape, scale):
        return jax.random.normal(k, shape, f32) * scale

    def gain(k, shape):
        return 1.0 + 0.02 * jax.random.normal(k, shape, f32)

    dt = jnp.exp(jax.random.uniform(ks[8], (DEPTH, DN_HEADS), f32, np.log(0.001), np.log(0.1)))
    return {
        "x": jax.random.normal(ks[0], (BATCH, SEQ, D_MODEL), f32),
        "ffn1_norm": gain(ks[1], (DEPTH, D_MODEL)),
        "ffn1_w_gate": nrm(ks[2], (DEPTH, D_MODEL, D_FF), D_MODEL ** -0.5),
        "ffn1_w_up": nrm(ks[3], (DEPTH, D_MODEL, D_FF), D_MODEL ** -0.5),
        "ffn1_w_down": nrm(ks[4], (DEPTH, D_FF, D_MODEL), D_FF ** -0.5),
        "mix_norm": gain(ks[5], (DEPTH, D_MODEL)),
        "w_in": nrm(ks[6], (DEPTH, D_MODEL, IN_COLS), D_MODEL ** -0.5),
        "conv_w": nrm(ks[7], (DEPTH, CONV_K, 3 * DN_WIDTH), CONV_K ** -0.5),
        "a_log": jnp.log(jax.random.uniform(ks[9], (DEPTH, DN_HEADS), f32, 1.0, 16.0)),
        "dt_bias": dt + jnp.log(-jnp.expm1(-dt)),
        "dn_norm": gain(ks[10], (DEPTH, DN_HEAD_DIM)),
        "sg_ln_g": gain(ks[11], (DEPTH, SG_WIDTH)),
        "sg_ln_b": nrm(ks[12], (DEPTH, SG_WIDTH), 0.02),
        "sg_w": nrm(ks[13], (DEPTH, SG_GROUPS, SG_CHUNK, SG_CHUNK), SG_CHUNK ** -0.5),
        "sg_b": gain(ks[14], (DEPTH, SG_GROUPS, SG_CHUNK)),
        "w_out": nrm(ks[15], (DEPTH, D_MIX, D_MODEL), D_MIX ** -0.5),
        "ffn2_norm": gain(ks[16], (DEPTH, D_MODEL)),
        "ffn2_w_gate": nrm(ks[17], (DEPTH, D_MODEL, D_FF), D_MODEL ** -0.5),
        "ffn2_w_up": nrm(ks[18], (DEPTH, D_MODEL, D_FF), D_MODEL ** -0.5),
        "ffn2_w_down": nrm(ks[19], (DEPTH, D_FF, D_MODEL), D_FF ** -0.5),
        "final_norm": gain(ks[20], (D_MODEL,)),
    }


def reference(x, ffn1_norm, ffn1_w_gate, ffn1_w_up, ffn1_w_down, mix_norm, w_in, conv_w, a_log,
              dt_bias, dn_norm, sg_ln_g, sg_ln_b, sg_w, sg_b, w_out, ffn2_norm, ffn2_w_gate,
              ffn2_w_up, ffn2_w_down, final_norm):
    for l in range(DEPTH):
        h = rmsnorm(x, ffn1_norm[l])
        x = x + 0.5 * swiglu(h, ffn1_w_gate[l], ffn1_w_up[l], ffn1_w_down[l])
        h = rmsnorm(x, mix_norm[l])
        x = x + hybrid_mixer(h, w_in[l], conv_w[l], a_log[l], dt_bias[l], dn_norm[l],
                             sg_ln_g[l], sg_ln_b[l], sg_w[l], sg_b[l], w_out[l])
        h = rmsnorm(x, ffn2_norm[l])
        x = x + 0.5 * swiglu(h, ffn2_w_gate[l], ffn2_w_up[l], ffn2_w_down[l])
    return rmsnorm(x, final_norm)
```

```python
import functools
import math

import jax
import jax.numpy as jnp
from jax import lax
from jax.experimental import pallas as pl
from jax.experimental.pallas import tpu as pltpu

F32 = jnp.float32
BF16 = jnp.bfloat16
EPS = 1e-6

LANES = 128
SG_GROUPS = 8
SG_CHUNK = 128
DN_HEADS = 4
DN_HEAD_DIM = 128
CONV_K = 4
CHUNK = 128
CONV_PAD = 8
FF_CHUNK = 256
TOKEN_TILE = 512
TIME_TILE = 512
VMEM_LIMIT = 58 * 1024 * 1024

N_LEVELS = 7
IDX_STRICT = N_LEVELS
IDX_INCL = N_LEVELS + 1
N_CONST = N_LEVELS + 2


def _mm(a, b):
    return jnp.dot(a.astype(BF16), b.astype(BF16), preferred_element_type=F32)


def _mm_nt(a, b):
    return lax.dot_general(a.astype(BF16), b.astype(BF16), (((1,), (1,)), ((), ())),
                           preferred_element_type=F32)


def _mm_tn(a, b):
    return lax.dot_general(a.astype(BF16), b.astype(BF16), (((0,), (0,)), ((), ())),
                           preferred_element_type=F32)


def _mm_f32(a, b):
    return jnp.dot(a, b, preferred_element_type=F32, precision=lax.Precision.HIGHEST)


def _rms(x, g):
    return x * lax.rsqrt(jnp.mean(x * x, axis=-1, keepdims=True) + EPS) * g


def _silu(x):
    return x * jax.nn.sigmoid(x)


def _swiglu(h_ref, wgu_ref, wd_ref, a_ref):
    h = h_ref[...]
    for c in range(wgu_ref.shape[0]):
        gu = jnp.dot(h, wgu_ref[c], preferred_element_type=F32)
        a_ref[:, c * FF_CHUNK:(c + 1) * FF_CHUNK] = (_silu(gu[:, :FF_CHUNK]) * gu[:, FF_CHUNK:]).astype(BF16)
    return jnp.dot(a_ref[...], wd_ref[...], preferred_element_type=F32)


def _ffn1_inproj_kernel(x_ref, g1_ref, wgu_ref, wd_ref, gm_ref, win_ref, wsm_ref,
                        x1_ref, proj_ref, small_ref, h_ref, a_ref):
    x = x_ref[...]
    h_ref[...] = _rms(x, g1_ref[...]).astype(BF16)
    x1 = x + 0.5 * _swiglu(h_ref, wgu_ref, wd_ref, a_ref)
    x1_ref[...] = x1
    h_ref[...] = _rms(x1, gm_ref[...]).astype(BF16)
    h = h_ref[...]
    n_main = proj_ref.shape[1]
    step = 4 * LANES
    for c in range(n_main // step):
        proj_ref[:, c * step:(c + 1) * step] = jnp.dot(h, win_ref[:, c * step:(c + 1) * step],
                                                       preferred_element_type=F32)
    small_ref[...] = jnp.dot(h, wsm_ref[...], preferred_element_type=F32)


def _outproj_ffn2_kernel(x1_ref, mix_ref, wout_ref, g2_ref, wgu_ref, wd_ref, gf_ref,
                         out_ref, h_ref, a_ref, *, final_norm):
    x2 = x1_ref[...] + jnp.dot(mix_ref[...], wout_ref[...], preferred_element_type=F32)
    h_ref[...] = _rms(x2, g2_ref[...]).astype(BF16)
    y = x2 + 0.5 * _swiglu(h_ref, wgu_ref, wd_ref, a_ref)
    out_ref[...] = _rms(y, gf_ref[...]) if final_norm else y


def _unit_lower_inverse_minus_identity(l_mat, const_ref):
    n = -(l_mat * const_ref[0])
    for lvl in range(1, N_LEVELS):
        y = l_mat * const_ref[lvl]
        z = y + _mm(n, y)
        n = n - (z + _mm(z, n))
    return n


def _mixer_kernel(proj_ref, small_ref, convw_ref, alog_ref, dtb_ref, dnn_ref, lng_ref, lnb_ref,
                  wpair_ref, sgbias_ref, const_ref, mix_ref,
                  state_ref, cbuf_ref, qkv_ref, beta_ref, g_ref, *, tt):
    sg_w = SG_GROUPS * (LANES // 2)
    dn_w = DN_HEADS * DN_HEAD_DIM
    n_chunks = tt // CHUNK

    @pl.when(pl.program_id(1) == 0)
    def _():
        state_ref[...] = jnp.zeros_like(state_ref)
        cbuf_ref[0:CONV_PAD, :] = jnp.zeros((CONV_PAD, cbuf_ref.shape[1]), F32)

    lane = lax.broadcasted_iota(jnp.int32, (1, LANES), 1)
    lo = lane < (LANES // 2)
    for n in range(tt // SG_CHUNK):
        rows = slice(n * SG_CHUNK, (n + 1) * SG_CHUNK)
        v = jax.nn.gelu(proj_ref[rows, sg_w:2 * sg_w], approximate=True)
        mu = jnp.mean(v, axis=-1, keepdims=True)
        vc = v - mu
        var = jnp.mean(vc * vc, axis=-1, keepdims=True)
        vln = vc * lax.rsqrt(var + EPS) * lng_ref[...] + lnb_ref[...]
        for p in range(SG_GROUPS // 2):
            cols = slice(p * LANES, (p + 1) * LANES)
            vp = vln[:, cols]
            rhs = jnp.concatenate([jnp.where(lo, vp, 0.0), jnp.where(lo, 0.0, vp)], axis=0)
            vs = _mm(wpair_ref[p], rhs) + sgbias_ref[:, cols]
            u = jax.nn.gelu(proj_ref[rows, cols], approximate=True)
            mix_ref[rows, cols] = (u * vs).astype(BF16)

    cbuf_ref[CONV_PAD:CONV_PAD + tt, :] = proj_ref[:, 2 * sg_w:2 * sg_w + 3 * dn_w]
    for n in range(n_chunks):
        base = CONV_PAD - (CONV_K - 1) + n * CHUNK
        acc = convw_ref[0:1, :] * cbuf_ref[base:base + CHUNK, :]
        for j in range(1, CONV_K):
            acc = acc + convw_ref[j:j + 1, :] * cbuf_ref[base + j:base + j + CHUNK, :]
        qkv_ref[n * CHUNK:(n + 1) * CHUNK, :] = _silu(acc)
    cbuf_ref[0:CONV_PAD, :] = cbuf_ref[tt:tt + CONV_PAD, :]

    sm = small_ref[...]
    beta_ref[...] = jax.nn.sigmoid(sm)
    xs = sm + dtb_ref[...]
    softplus = jnp.maximum(xs, 0.0) + jnp.log(1.0 + jnp.exp(-jnp.abs(xs)))
    g_ref[...] = -jnp.exp(alog_ref[...]) * softplus

    strict = const_ref[IDX_STRICT]
    incl = const_ref[IDX_INCL]
    scale = DN_HEAD_DIM ** -0.5

    def chunk_body(c, carry):
        r = pl.multiple_of(c * CHUNK, CHUNK)
        rows = pl.ds(r, CHUNK)
        gc_all = _mm_f32(incl, g_ref[rows, :])
        gc_all_t = gc_all.T
        e_all = jnp.exp(gc_all)
        beta_all = beta_ref[rows, :]
        for h in range(DN_HEADS):
            hc = slice(h * DN_HEAD_DIM, (h + 1) * DN_HEAD_DIM)
            q = qkv_ref[rows, hc]
            k = qkv_ref[rows, dn_w + h * DN_HEAD_DIM:dn_w + (h + 1) * DN_HEAD_DIM]
            v = qkv_ref[rows, 2 * dn_w + h * DN_HEAD_DIM:2 * dn_w + (h + 1) * DN_HEAD_DIM]
            q = q * (lax.rsqrt(jnp.sum(q * q, axis=-1, keepdims=True) + EPS) * scale)
            k = k * lax.rsqrt(jnp.sum(k * k, axis=-1, keepdims=True) + EPS)
            beta = beta_all[:, h:h + 1]
            gc = gc_all[:, DN_HEADS + h:DN_HEADS + h + 1]
            gc_row = gc_all_t[DN_HEADS + h:DN_HEADS + h + 1, :]
            e = e_all[:, DN_HEADS + h:DN_HEADS + h + 1]
            gc_last = gc_all[CHUNK - 1:CHUNK, DN_HEADS + h:DN_HEADS + h + 1]
            decay = jnp.exp(jnp.minimum(gc - gc_row, 0.0))
            kb = k * beta
            vb = v * beta
            l_mat = _mm_nt(kb, k) * decay * strict
            n_inv = _unit_lower_inverse_minus_identity(l_mat, const_ref)
            rhs = jnp.concatenate([vb, kb * e], axis=1)
            sol = rhs + _mm(n_inv, rhs)
            u_wy = sol[:, :DN_HEAD_DIM]
            w_wy = sol[:, DN_HEAD_DIM:]
            qk = _mm_nt(q, k) * decay * incl
            s = state_ref[h]
            v_new = u_wy - _mm(w_wy, s)
            o = _mm(q * e, s) + _mm(qk, v_new)
            k_dec = k * jnp.exp(gc_last - gc)
            state_ref[h] = s * jnp.exp(gc_last) + _mm_tn(k_dec, v_new)
            o = o * lax.rsqrt(jnp.mean(o * o, axis=-1, keepdims=True) + EPS) * dnn_ref[...]
            z = proj_ref[rows, 2 * sg_w + 3 * dn_w + h * DN_HEAD_DIM:2 * sg_w + 3 * dn_w + (h + 1) * DN_HEAD_DIM]
            mix_ref[rows, sg_w + h * DN_HEAD_DIM:sg_w + (h + 1) * DN_HEAD_DIM] = (o * _silu(z)).astype(BF16)
        return carry

    lax.fori_loop(0, n_chunks, chunk_body, 0)


def _resident(shape):
    nd = len(shape)
    return pl.BlockSpec(shape, lambda *_: (0,) * nd, pipeline_mode=pl.Buffered(1))


def _prep_gate_up(w_gate, w_up):
    d, f = w_gate.shape
    nc = f // FF_CHUNK
    wg = w_gate.reshape(d, nc, FF_CHUNK)
    wu = w_up.reshape(d, nc, FF_CHUNK)
    return jnp.concatenate([wg, wu], axis=-1).transpose(1, 0, 2).astype(BF16)


def _mask_pack():
    i = jnp.arange(CHUNK)[:, None]
    j = jnp.arange(CHUNK)[None, :]
    rows = []
    for lvl in range(N_LEVELS):
        b = 1 << lvl
        rows.append((i // (2 * b) == j // (2 * b)) & ((i // b) % 2 == 1) & ((j // b) % 2 == 0))
    rows.append(i > j)
    rows.append(i >= j)
    return jnp.stack(rows).astype(F32)


def _ffn1_inproj(x2d, g1, wgu, wd, gm, win, wsm, tm):
    m, d = x2d.shape
    n_main = win.shape[1]
    tok = lambda w: pl.BlockSpec((tm, w), lambda i: (i, 0))
    return pl.pallas_call(
        _ffn1_inproj_kernel,
        grid=(m // tm,),
        in_specs=[tok(d), _resident(g1.shape), _resident(wgu.shape), _resident(wd.shape),
                  _resident(gm.shape), _resident(win.shape), _resident(wsm.shape)],
        out_specs=[tok(d), tok(n_main), tok(LANES)],
        out_shape=[jax.ShapeDtypeStruct((m, d), F32), jax.ShapeDtypeStruct((m, n_main), F32),
                   jax.ShapeDtypeStruct((m, LANES), F32)],
        scratch_shapes=[pltpu.VMEM((tm, d), BF16), pltpu.VMEM((tm, wd.shape[0]), BF16)],
        compiler_params=pltpu.CompilerParams(dimension_semantics=("arbitrary",), vmem_limit_bytes=VMEM_LIMIT),
        name="ffn1_inproj",
    )(x2d, g1, wgu, wd, gm, win, wsm)


def _outproj_ffn2(x1, mix, wout, g2, wgu, wd, gf, tm, final_norm):
    m, d = x1.shape
    tok = lambda w: pl.BlockSpec((tm, w), lambda i: (i, 0))
    return pl.pallas_call(
        functools.partial(_outproj_ffn2_kernel, final_norm=final_norm),
        grid=(m // tm,),
        in_specs=[tok(d), tok(mix.shape[1]), _resident(wout.shape), _resident(g2.shape),
                  _resident(wgu.shape), _resident(wd.shape), _resident(gf.shape)],
        out_specs=tok(d),
        out_shape=jax.ShapeDtypeStruct((m, d), F32),
        scratch_shapes=[pltpu.VMEM((tm, d), BF16), pltpu.VMEM((tm, wd.shape[0]), BF16)],
        compiler_params=pltpu.CompilerParams(dimension_semantics=("arbitrary",), vmem_limit_bytes=VMEM_LIMIT),
        name="outproj_ffn2",
    )(x1, mix, wout, g2, wgu, wd, gf)


def _mixer(proj, small, convw, alog, dtb, dnn, lng, lnb, wpair, sgbias, consts, bsz, t_len, tt):
    m, n_main = proj.shape
    nt = t_len // tt
    d_mix = SG_GROUPS * (LANES // 2) + DN_HEADS * DN_HEAD_DIM
    qkv_w = 3 * DN_HEADS * DN_HEAD_DIM
    tok = lambda w: pl.BlockSpec((tt, w), lambda b, t: (b * nt + t, 0))
    params = (convw, alog, dtb, dnn, lng, lnb, wpair, sgbias, consts)
    return pl.pallas_call(
        functools.partial(_mixer_kernel, tt=tt),
        grid=(bsz, nt),
        in_specs=[tok(n_main), tok(LANES)] + [_resident(p.shape) for p in params],
        out_specs=tok(d_mix),
        out_shape=jax.ShapeDtypeStruct((m, d_mix), BF16),
        scratch_shapes=[pltpu.VMEM((DN_HEADS, DN_HEAD_DIM, DN_HEAD_DIM), F32),
                        pltpu.VMEM((tt + CONV_PAD, qkv_w), F32),
                        pltpu.VMEM((tt, qkv_w), F32),
                        pltpu.VMEM((tt, LANES), F32),
                        pltpu.VMEM((tt, LANES), F32)],
        compiler_params=pltpu.CompilerParams(dimension_semantics=("arbitrary", "arbitrary"),
                                             vmem_limit_bytes=VMEM_LIMIT),
        name="mixer",
    )(proj, small, *params)


def _pad_lanes(v, offset):
    return jnp.zeros((1, LANES), F32).at[0, offset:offset + v.shape[0]].set(v.astype(F32))


def kernel(x, ffn1_norm, ffn1_w_gate, ffn1_w_up, ffn1_w_down, mix_norm, w_in, conv_w, a_log, dt_bias, dn_norm,
           sg_ln_g, sg_ln_b, sg_w, sg_b, w_out, ffn2_norm, ffn2_w_gate, ffn2_w_up, ffn2_w_down, final_norm):
    bsz, t_len, d = x.shape
    depth = ffn1_norm.shape[0]
    m = bsz * t_len
    tm = math.gcd(TOKEN_TILE, m)
    tt = math.gcd(TIME_TILE, t_len)
    assert t_len % CHUNK == 0 and SG_CHUNK == CHUNK and sg_w.shape[-1] == SG_CHUNK
    sg_width = SG_GROUPS * (LANES // 2)
    n_main = 2 * sg_width + 4 * DN_HEADS * DN_HEAD_DIM
    assert w_in.shape[-1] == n_main + 2 * DN_HEADS

    consts = _mask_pack()
    causal = consts[IDX_INCL]
    row = lambda v: v.astype(F32).reshape(1, -1)
    xf = x.reshape(m, d)
    for l in range(depth):
        wsm = jnp.zeros((d, LANES), F32).at[:, :2 * DN_HEADS].set(w_in[l][:, n_main:]).astype(BF16)
        wc = sg_w[l] * causal
        wpair = jnp.concatenate([wc[0::2], wc[1::2]], axis=-1).astype(BF16)
        sgbias = jnp.repeat(sg_b[l].T.astype(F32), LANES // 2, axis=1)
        x1, proj, small = _ffn1_inproj(
            xf, row(ffn1_norm[l]), _prep_gate_up(ffn1_w_gate[l], ffn1_w_up[l]), ffn1_w_down[l].astype(BF16),
            row(mix_norm[l]), w_in[l][:, :n_main].astype(BF16), wsm, tm)
        mix = _mixer(proj, small, conv_w[l].astype(F32), _pad_lanes(a_log[l], DN_HEADS),
                     _pad_lanes(dt_bias[l], DN_HEADS), row(dn_norm[l]), row(sg_ln_g[l]), row(sg_ln_b[l]),
                     wpair, sgbias, consts, bsz, t_len, tt)
        last = l == depth - 1
        xf = _outproj_ffn2(
            x1, mix, w_out[l].astype(BF16), row(ffn2_norm[l]),
            _prep_gate_up(ffn2_w_gate[l], ffn2_w_up[l]), ffn2_w_down[l].astype(BF16),
            row(final_norm) if last else row(ffn2_norm[l]), tm, last)
    return xf.reshape(bsz, t_len, d)
```

```python
import functools
import math

import jax
import jax.numpy as jnp
from jax import lax
from jax.experimental import pallas as pl
from jax.experimental.pallas import tpu as pltpu

F32 = jnp.float32
BF16 = jnp.bfloat16
EPS = 1e-6

LANES = 128
SG_GROUPS = 8
SG_CHUNK = 128
SG_WIDTH = SG_GROUPS * (LANES // 2)
DN_HEADS = 4
DN_HEAD_DIM = 128
DN_WIDTH = DN_HEADS * DN_HEAD_DIM
CONV_K = 4
CHUNK = 128
CONV_PAD = 8
FF_CHUNK = 256
TOKEN_TILE = 512
TIME_TILE = 512
VMEM_LIMIT = 58 * 1024 * 1024

N_LEVELS = 7
IDX_STRICT = N_LEVELS
IDX_INCL = N_LEVELS + 1
IDX_EYE = N_LEVELS + 2
N_CONST = N_LEVELS + 3


def _mm(a, b):
    return jnp.dot(a, b, preferred_element_type=F32)


def _mm_f32(a, b):
    return jnp.dot(a, b, preferred_element_type=F32, precision=lax.Precision.HIGHEST)


def _rms(x, g):
    return x * lax.rsqrt(jnp.mean(x * x, axis=-1, keepdims=True) + EPS) * g


def _silu(x):
    return x * jax.nn.sigmoid(x)


def _gelu(x):
    return jax.nn.gelu(x, approximate=True)


def _swiglu(h_ref, wgu_ref, wd_ref, a_ref):
    h = h_ref[...]
    for c in range(wgu_ref.shape[0]):
        gu = _mm(h, wgu_ref[c])
        a_ref[:, c * FF_CHUNK:(c + 1) * FF_CHUNK] = (_silu(gu[:, :FF_CHUNK]) * gu[:, FF_CHUNK:]).astype(BF16)
    return _mm(a_ref[...], wd_ref[...])


def _ffn1_inproj_kernel(x_ref, g1_ref, wgu_ref, wd_ref, gm_ref, win_ref, wsm_ref, convw_ref, alog_ref, dtb_ref,
                        lng_ref, lnb_ref,
                        x1_ref, u_ref, vln_ref, q_ref, k_ref, kb_ref, vb_ref, zs_ref, bg_ref,
                        h_ref, a_ref, cbuf_ref, *, tiles_per_seq):
    tm = x_ref.shape[0]
    x = x_ref[...]
    h_ref[...] = _rms(x, g1_ref[...]).astype(BF16)
    x1 = x + 0.5 * _swiglu(h_ref, wgu_ref, wd_ref, a_ref)
    x1_ref[...] = x1
    h_ref[...] = _rms(x1, gm_ref[...]).astype(BF16)
    h = h_ref[...]

    sm = _mm(h, wsm_ref[...])
    beta_all = jax.nn.sigmoid(sm)
    xs = sm + dtb_ref[...]
    softplus = jnp.maximum(xs, 0.0) + jnp.log(1.0 + jnp.exp(-jnp.abs(xs)))
    lane = lax.broadcasted_iota(jnp.int32, (1, LANES), 1)
    bg_ref[...] = jnp.where(lane < DN_HEADS, beta_all, -jnp.exp(alog_ref[...]) * softplus)

    u_ref[...] = _gelu(_mm(h, win_ref[:, 0:SG_WIDTH])).astype(BF16)
    v = _gelu(_mm(h, win_ref[:, SG_WIDTH:2 * SG_WIDTH]))
    vc = v - jnp.mean(v, axis=-1, keepdims=True)
    var = jnp.mean(vc * vc, axis=-1, keepdims=True)
    vln_ref[...] = (vc * lax.rsqrt(var + EPS) * lng_ref[...] + lnb_ref[...]).astype(BF16)

    @pl.when(pl.program_id(0) % tiles_per_seq == 0)
    def _():
        cbuf_ref[0:CONV_PAD, :] = jnp.zeros((CONV_PAD, cbuf_ref.shape[1]), F32)

    off = 2 * SG_WIDTH
    for c in range(3):
        cbuf_ref[CONV_PAD:CONV_PAD + tm, c * DN_WIDTH:(c + 1) * DN_WIDTH] = _mm(
            h, win_ref[:, off + c * DN_WIDTH:off + (c + 1) * DN_WIDTH])
    base = CONV_PAD - (CONV_K - 1)

    def conv_silu(cols):
        acc = convw_ref[0:1, cols] * cbuf_ref[base:base + tm, cols]
        for j in range(1, CONV_K):
            acc = acc + convw_ref[j:j + 1, cols] * cbuf_ref[base + j:base + j + tm, cols]
        return _silu(acc)

    scale = DN_HEAD_DIM ** -0.5
    for hd in range(DN_HEADS):
        hc = slice(hd * DN_HEAD_DIM, (hd + 1) * DN_HEAD_DIM)
        beta = beta_all[:, hd:hd + 1]
        q = conv_silu(hc)
        q_ref[:, hc] = (q * (lax.rsqrt(jnp.sum(q * q, axis=-1, keepdims=True) + EPS) * scale)).astype(BF16)
        k = conv_silu(slice(DN_WIDTH + hd * DN_HEAD_DIM, DN_WIDTH + (hd + 1) * DN_HEAD_DIM))
        k = k * lax.rsqrt(jnp.sum(k * k, axis=-1, keepdims=True) + EPS)
        k_ref[:, hc] = k.astype(BF16)
        kb_ref[:, hc] = (k * beta).astype(BF16)
        v = conv_silu(slice(2 * DN_WIDTH + hd * DN_HEAD_DIM, 2 * DN_WIDTH + (hd + 1) * DN_HEAD_DIM))
        vb_ref[:, hc] = (v * beta).astype(BF16)
    cbuf_ref[0:CONV_PAD, :] = cbuf_ref[tm:tm + CONV_PAD, :]

    off = 2 * SG_WIDTH + 3 * DN_WIDTH
    zs_ref[...] = _silu(_mm(h, win_ref[:, off:off + DN_WIDTH])).astype(BF16)


def _outproj_ffn2_kernel(x1_ref, mix_ref, wout_ref, g2_ref, wgu_ref, wd_ref, gf_ref,
                         out_ref, h_ref, a_ref, *, final_norm):
    x2 = x1_ref[...] + _mm(mix_ref[...], wout_ref[...])
    h_ref[...] = _rms(x2, g2_ref[...]).astype(BF16)
    y = x2 + 0.5 * _swiglu(h_ref, wgu_ref, wd_ref, a_ref)
    out_ref[...] = _rms(y, gf_ref[...]) if final_norm else y


def _mixer_kernel(u_ref, vln_ref, q_ref, k_ref, kb_ref, vb_ref, zs_ref, bg_ref, dnn_ref,
                  wpair_ref, sgbias_ref, const_ref, mix_ref,
                  state_ref, l_ref, n_ref, nb_ref, z_ref, kt_ref, qk_ref, uwy_ref, w_ref, *, tt):
    n_chunks = tt // CHUNK
    units = [(n, hd) for n in range(n_chunks) for hd in range(DN_HEADS)]

    @pl.when(pl.program_id(1) == 0)
    def _():
        state_ref[...] = jnp.zeros_like(state_ref)

    lane = lax.broadcasted_iota(jnp.int32, (1, LANES), 1)
    lo = lane < (LANES // 2)
    zero = jnp.zeros((), BF16)
    for n in range(tt // SG_CHUNK):
        rows = slice(n * SG_CHUNK, (n + 1) * SG_CHUNK)
        for p in range(SG_GROUPS // 2):
            cols = slice(p * LANES, (p + 1) * LANES)
            vp = vln_ref[rows, cols]
            rhs = jnp.concatenate([jnp.where(lo, vp, zero), jnp.where(lo, zero, vp)], axis=0)
            vs = _mm(wpair_ref[p], rhs) + sgbias_ref[:, cols]
            mix_ref[rows, cols] = (u_ref[rows, cols].astype(F32) * vs).astype(BF16)

    strict = const_ref[IDX_STRICT]
    incl = const_ref[IDX_INCL]
    eye = const_ref[IDX_EYE]

    gcs, gcts = [], []
    for n in range(n_chunks):
        gc_all = _mm_f32(incl, bg_ref[n * CHUNK:(n + 1) * CHUNK, :])
        gcs.append(gc_all)
        gcts.append(gc_all.T)

    def cols_of(n, hd):
        return slice(n * CHUNK, (n + 1) * CHUNK), slice(hd * DN_HEAD_DIM, (hd + 1) * DN_HEAD_DIM)

    for ui, (n, hd) in enumerate(units):
        rows, hc = cols_of(n, hd)
        gc = gcs[n][:, DN_HEADS + hd:DN_HEADS + hd + 1]
        gc_row = gcts[n][DN_HEADS + hd:DN_HEADS + hd + 1, :]
        decay = jnp.exp(jnp.minimum(gc - gc_row, 0.0))
        kt = k_ref[rows, hc].T
        kt_ref[ui] = kt
        a = _mm(jnp.concatenate([kb_ref[rows, hc], q_ref[rows, hc]], axis=0), kt)
        l_mat = a[:CHUNK] * decay * strict
        l_ref[ui] = l_mat
        qk_ref[ui] = (a[CHUNK:] * decay * incl).astype(BF16)
        n0 = -(l_mat * const_ref[0])
        n_ref[ui] = n0
        nb_ref[ui] = n0.astype(BF16)

    for lvl in range(1, N_LEVELS):
        mask = const_ref[lvl]
        for ui in range(len(units)):
            y = l_ref[ui] * mask
            z_ref[ui] = y + _mm(nb_ref[ui], y.astype(BF16))
        for ui in range(len(units)):
            z = z_ref[ui]
            nn = n_ref[ui] - z - _mm(z.astype(BF16), nb_ref[ui])
            n_ref[ui] = nn
            nb_ref[ui] = nn.astype(BF16)

    for ui, (n, hd) in enumerate(units):
        rows, hc = cols_of(n, hd)
        t_mat = n_ref[ui] + eye
        e_row = jnp.exp(gcts[n][DN_HEADS + hd:DN_HEADS + hd + 1, :])
        uwy_ref[ui] = _mm(t_mat.astype(BF16), vb_ref[rows, hc])
        w_ref[ui] = _mm((t_mat * e_row).astype(BF16), kb_ref[rows, hc]).astype(BF16)

    for n in range(n_chunks):
        for hd in range(DN_HEADS):
            ui = n * DN_HEADS + hd
            rows, hc = cols_of(n, hd)
            gc = gcs[n][:, DN_HEADS + hd:DN_HEADS + hd + 1]
            gc_last = gcs[n][CHUNK - 1:CHUNK, DN_HEADS + hd:DN_HEADS + hd + 1]
            s = state_ref[hd]
            p = _mm(jnp.concatenate([w_ref[ui], q_ref[rows, hc]], axis=0), s.astype(BF16))
            v_new = uwy_ref[ui] - p[:CHUNK]
            o = p[CHUNK:] * jnp.exp(gc) + _mm(qk_ref[ui], v_new.astype(BF16))
            k_dec_v = (v_new * jnp.exp(gc_last - gc)).astype(BF16)
            state_ref[hd] = s * jnp.exp(gc_last) + _mm(kt_ref[ui], k_dec_v)
            o = o * lax.rsqrt(jnp.mean(o * o, axis=-1, keepdims=True) + EPS) * dnn_ref[...]
            mix_ref[rows, SG_WIDTH + hd * DN_HEAD_DIM:SG_WIDTH + (hd + 1) * DN_HEAD_DIM] = (
                o * zs_ref[rows, hc].astype(F32)).astype(BF16)


def _resident(shape):
    nd = len(shape)
    return pl.BlockSpec(shape, lambda *_: (0,) * nd, pipeline_mode=pl.Buffered(1))


def _prep_gate_up(w_gate, w_up):
    d, f = w_gate.shape
    nc = f // FF_CHUNK
    wg = w_gate.reshape(d, nc, FF_CHUNK)
    wu = w_up.reshape(d, nc, FF_CHUNK)
    return jnp.concatenate([wg, wu], axis=-1).transpose(1, 0, 2).astype(BF16)


def _mask_pack():
    i = jnp.arange(CHUNK)[:, None]
    j = jnp.arange(CHUNK)[None, :]
    rows = []
    for lvl in range(N_LEVELS):
        b = 1 << lvl
        rows.append((i // (2 * b) == j // (2 * b)) & ((i // b) % 2 == 1) & ((j // b) % 2 == 0))
    rows.append(i > j)
    rows.append(i >= j)
    rows.append(i == j)
    return jnp.stack(rows).astype(F32)


def _ffn1_inproj(x2d, params, tm, t_len):
    m, d = x2d.shape
    tok = lambda w: pl.BlockSpec((tm, w), lambda i: (i, 0))
    bf = lambda w: jax.ShapeDtypeStruct((m, w), BF16)
    wd = params[2]
    return pl.pallas_call(
        functools.partial(_ffn1_inproj_kernel, tiles_per_seq=t_len // tm),
        grid=(m // tm,),
        in_specs=[tok(d)] + [_resident(p.shape) for p in params],
        out_specs=[tok(d), tok(SG_WIDTH), tok(SG_WIDTH)] + [tok(DN_WIDTH)] * 5 + [tok(LANES)],
        out_shape=[jax.ShapeDtypeStruct((m, d), F32), bf(SG_WIDTH), bf(SG_WIDTH)] + [bf(DN_WIDTH)] * 5
                  + [jax.ShapeDtypeStruct((m, LANES), F32)],
        scratch_shapes=[pltpu.VMEM((tm, d), BF16), pltpu.VMEM((tm, wd.shape[0]), BF16),
                        pltpu.VMEM((tm + CONV_PAD, 3 * DN_WIDTH), F32)],
        compiler_params=pltpu.CompilerParams(dimension_semantics=("arbitrary",), vmem_limit_bytes=VMEM_LIMIT),
        name="ffn1_inproj",
    )(x2d, *params)


def _outproj_ffn2(x1, mix, wout, g2, wgu, wd, gf, tm, final_norm):
    m, d = x1.shape
    tok = lambda w: pl.BlockSpec((tm, w), lambda i: (i, 0))
    return pl.pallas_call(
        functools.partial(_outproj_ffn2_kernel, final_norm=final_norm),
        grid=(m // tm,),
        in_specs=[tok(d), tok(mix.shape[1]), _resident(wout.shape), _resident(g2.shape),
                  _resident(wgu.shape), _resident(wd.shape), _resident(gf.shape)],
        out_specs=tok(d),
        out_shape=jax.ShapeDtypeStruct((m, d), F32),
        scratch_shapes=[pltpu.VMEM((tm, d), BF16), pltpu.VMEM((tm, wd.shape[0]), BF16)],
        compiler_params=pltpu.CompilerParams(dimension_semantics=("arbitrary",), vmem_limit_bytes=VMEM_LIMIT),
        name="outproj_ffn2",
    )(x1, mix, wout, g2, wgu, wd, gf)


def _mixer(acts, params, bsz, t_len, tt):
    m = acts[0].shape[0]
    nt = t_len // tt
    n_units = (tt // CHUNK) * DN_HEADS
    tok = lambda w: pl.BlockSpec((tt, w), lambda b, t: (b * nt + t, 0))
    mat = lambda dt: pltpu.VMEM((n_units, CHUNK, CHUNK), dt)
    return pl.pallas_call(
        functools.partial(_mixer_kernel, tt=tt),
        grid=(bsz, nt),
        in_specs=[tok(a.shape[1]) for a in acts] + [_resident(p.shape) for p in params],
        out_specs=tok(SG_WIDTH + DN_WIDTH),
        out_shape=jax.ShapeDtypeStruct((m, SG_WIDTH + DN_WIDTH), BF16),
        scratch_shapes=[pltpu.VMEM((DN_HEADS, DN_HEAD_DIM, DN_HEAD_DIM), F32),
                        mat(F32), mat(F32), mat(BF16), mat(F32), mat(BF16), mat(BF16), mat(F32), mat(BF16)],
        compiler_params=pltpu.CompilerParams(dimension_semantics=("arbitrary", "arbitrary"),
                                             vmem_limit_bytes=VMEM_LIMIT),
        name="mixer",
    )(*acts, *params)


def _pad_lanes(v, offset):
    return jnp.zeros((1, LANES), F32).at[0, offset:offset + v.shape[0]].set(v.astype(F32))


def kernel(x, ffn1_norm, ffn1_w_gate, ffn1_w_up, ffn1_w_down, mix_norm, w_in, conv_w, a_log, dt_bias, dn_norm,
           sg_ln_g, sg_ln_b, sg_w, sg_b, w_out, ffn2_norm, ffn2_w_gate, ffn2_w_up, ffn2_w_down, final_norm):
    bsz, t_len, d = x.shape
    depth = ffn1_norm.shape[0]
    m = bsz * t_len
    tm = math.gcd(TOKEN_TILE, t_len)
    tt = math.gcd(TIME_TILE, t_len)
    assert t_len % CHUNK == 0 and SG_CHUNK == CHUNK and sg_w.shape[-1] == SG_CHUNK
    n_main = 2 * SG_WIDTH + 4 * DN_WIDTH
    assert w_in.shape[-1] == n_main + 2 * DN_HEADS

    consts = _mask_pack()
    causal = consts[IDX_INCL]
    row = lambda v: v.astype(F32).reshape(1, -1)
    xf = x.reshape(m, d)
    for l in range(depth):
        wsm = jnp.zeros((d, LANES), F32).at[:, :2 * DN_HEADS].set(w_in[l][:, n_main:]).astype(BF16)
        wc = sg_w[l] * causal
        wpair = jnp.concatenate([wc[0::2], wc[1::2]], axis=-1).astype(BF16)
        sgbias = jnp.repeat(sg_b[l].T.astype(F32), LANES // 2, axis=1)
        outs = _ffn1_inproj(
            xf, (row(ffn1_norm[l]), _prep_gate_up(ffn1_w_gate[l], ffn1_w_up[l]), ffn1_w_down[l].astype(BF16),
                 row(mix_norm[l]), w_in[l][:, :n_main].astype(BF16), wsm, conv_w[l].astype(F32),
                 _pad_lanes(a_log[l], DN_HEADS), _pad_lanes(dt_bias[l], DN_HEADS),
                 row(sg_ln_g[l]), row(sg_ln_b[l])), tm, t_len)
        x1, acts = outs[0], outs[1:]
        mix = _mixer(acts, (row(dn_norm[l]), wpair, sgbias, consts), bsz, t_len, tt)
        last = l == depth - 1
        xf = _outproj_ffn2(
            x1, mix, w_out[l].astype(BF16), row(ffn2_norm[l]),
            _prep_gate_up(ffn2_w_gate[l], ffn2_w_up[l]), ffn2_w_down[l].astype(BF16),
            row(final_norm) if last else row(ffn2_norm[l]), tm, last)
    return xf.reshape(bsz, t_len, d)
```

```python
import functools
import math

import jax
import jax.numpy as jnp
from jax import lax
from jax.experimental import pallas as pl
from jax.experimental.pallas import tpu as pltpu

F32 = jnp.float32
BF16 = jnp.bfloat16
EPS = 1e-6

LANES = 128
SG_GROUPS = 8
SG_CHUNK = 128
SG_WIDTH = SG_GROUPS * (LANES // 2)
DN_HEADS = 4
DN_HEAD_DIM = 128
DN_WIDTH = DN_HEADS * DN_HEAD_DIM
CONV_K = 4
CHUNK = 128
CONV_PAD = 8
FF_CHUNK = 256
TOKEN_TILE = 512
ROW_BLOCK = 64
TIME_TILE = 512
VMEM_LIMIT = 58 * 1024 * 1024

N_LEVELS = 7
IDX_STRICT = N_LEVELS
IDX_INCL = N_LEVELS + 1
IDX_EYE = N_LEVELS + 2
N_CONST = N_LEVELS + 3


def _mm(a, b):
    return jnp.dot(a, b, preferred_element_type=F32)


def _mm_f32(a, b):
    return jnp.dot(a, b, preferred_element_type=F32, precision=lax.Precision.HIGHEST)


def _rms(x, g):
    return x * lax.rsqrt(jnp.mean(x * x, axis=-1, keepdims=True) + EPS) * g


def _silu(x):
    return x * jax.nn.sigmoid(x)


def _gelu(x):
    return jax.nn.gelu(x, approximate=True)


def _swiglu(h_ref, wgu_ref, wd_ref, a_ref):
    h = h_ref[...]
    for c in range(wgu_ref.shape[0]):
        gu = _mm(h, wgu_ref[c])
        a_ref[:, c * FF_CHUNK:(c + 1) * FF_CHUNK] = (_silu(gu[:, :FF_CHUNK]) * gu[:, FF_CHUNK:]).astype(BF16)
    return _mm(a_ref[...], wd_ref[...])


def _ffn1_inproj_kernel(x_ref, g1_ref, wgu_ref, wd_ref, gm_ref, win_ref, wsm_ref, convw_ref, alog_ref, dtb_ref,
                        lng_ref, lnb_ref,
                        x1_ref, u_ref, vln_ref, q_ref, k_ref, kb_ref, vb_ref, zs_ref, bg_ref,
                        h_ref, h2_ref, a_ref, cbuf_ref, st_ref, *, tiles_per_seq):
    tm = x_ref.shape[0]
    step = pl.program_id(0)

    @pl.when(step == 0)
    def _():
        h2_ref[...] = jnp.zeros_like(h2_ref)
        cbuf_ref[0:CONV_PAD, :] = jnp.zeros((CONV_PAD, cbuf_ref.shape[1]), F32)

    seq_start = (step - 1) % tiles_per_seq == 0
    cbuf_ref[0:CONV_PAD, :] = jnp.where(seq_start, 0.0, cbuf_ref[0:CONV_PAD, :])

    x = x_ref[...]
    h_ref[...] = _rms(x, g1_ref[...]).astype(BF16)
    h1 = h_ref[...]
    h2 = h2_ref[...]
    row_blocks = [slice(rb * ROW_BLOCK, (rb + 1) * ROW_BLOCK) for rb in range(tm // ROW_BLOCK)]
    lane = lax.broadcasted_iota(jnp.int32, (1, LANES), 1)
    scale = DN_HEAD_DIM ** -0.5

    def ffn_chunk(c):
        gu = _mm(h1, wgu_ref[c])
        a_ref[:, c * FF_CHUNK:(c + 1) * FF_CHUNK] = (_silu(gu[:, :FF_CHUNK]) * gu[:, FF_CHUNK:]).astype(BF16)

    def gates():
        st_ref[:, 3 * SG_WIDTH:] = _mm(h2, wsm_ref[...])
        for rows in row_blocks:
            sm = st_ref[rows, 3 * SG_WIDTH:]
            xs = sm + dtb_ref[...]
            softplus = jnp.maximum(xs, 0.0) + jnp.log(1.0 + jnp.exp(-jnp.abs(xs)))
            bg_ref[rows, :] = jnp.where(lane < DN_HEADS, jax.nn.sigmoid(sm), -jnp.exp(alog_ref[...]) * softplus)

    def conv_silu(rb, cols):
        r0 = rb * ROW_BLOCK
        xin = cbuf_ref[r0:r0 + CONV_PAD + ROW_BLOCK, cols]
        acc = convw_ref[CONV_K - 1:CONV_K, cols] * xin
        for s in range(1, CONV_K):
            acc = acc + convw_ref[CONV_K - 1 - s:CONV_K - s, cols] * pltpu.roll(xin, s, 0)
        return _silu(acc[CONV_PAD:])

    def dn_part(part):
        src = 2 * SG_WIDTH + part * DN_WIDTH
        cbuf_ref[CONV_PAD:CONV_PAD + tm, part * DN_WIDTH:(part + 1) * DN_WIDTH] = _mm(
            h2, win_ref[:, src:src + DN_WIDTH])
        for rb, rows in enumerate(row_blocks):
            for hd in range(DN_HEADS):
                hc = slice(hd * DN_HEAD_DIM, (hd + 1) * DN_HEAD_DIM)
                y = conv_silu(rb, slice(part * DN_WIDTH + hd * DN_HEAD_DIM, part * DN_WIDTH + (hd + 1) * DN_HEAD_DIM))
                if part == 0:
                    q_ref[rows, hc] = (y * (lax.rsqrt(jnp.sum(y * y, axis=-1, keepdims=True) + EPS) * scale)
                                       ).astype(BF16)
                elif part == 1:
                    k = y * lax.rsqrt(jnp.sum(y * y, axis=-1, keepdims=True) + EPS)
                    k_ref[rows, hc] = k.astype(BF16)
                    kb_ref[rows, hc] = (k * bg_ref[rows, hd:hd + 1]).astype(BF16)
                else:
                    vb_ref[rows, hc] = (y * bg_ref[rows, hd:hd + 1]).astype(BF16)

    def sg_u():
        st_ref[:, 0:SG_WIDTH] = _mm(h2, win_ref[:, 0:SG_WIDTH])
        for rows in row_blocks:
            u_ref[rows, :] = _gelu(st_ref[rows, 0:SG_WIDTH]).astype(BF16)

    def sg_v():
        st_ref[:, SG_WIDTH:2 * SG_WIDTH] = _mm(h2, win_ref[:, SG_WIDTH:2 * SG_WIDTH])
        for rows in row_blocks:
            v = _gelu(st_ref[rows, SG_WIDTH:2 * SG_WIDTH])
            vc = v - jnp.mean(v, axis=-1, keepdims=True)
            var = jnp.mean(vc * vc, axis=-1, keepdims=True)
            vln_ref[rows, :] = (vc * lax.rsqrt(var + EPS) * lng_ref[...] + lnb_ref[...]).astype(BF16)

    def dn_z():
        src = 2 * SG_WIDTH + 3 * DN_WIDTH
        st_ref[:, 2 * SG_WIDTH:3 * SG_WIDTH] = _mm(h2, win_ref[:, src:src + DN_WIDTH])
        for rows in row_blocks:
            zs_ref[rows, :] = _silu(st_ref[rows, 2 * SG_WIDTH:3 * SG_WIDTH]).astype(BF16)

    tasks = [gates, functools.partial(dn_part, 0), functools.partial(dn_part, 1), functools.partial(dn_part, 2),
             sg_u, sg_v, dn_z]
    n_ff = wgu_ref.shape[0]
    for t, task in enumerate(tasks):
        task()
        for c in range(n_ff):
            if c * len(tasks) // n_ff == t:
                ffn_chunk(c)
    cbuf_ref[0:CONV_PAD, :] = cbuf_ref[tm:tm + CONV_PAD, :]

    x1 = x + 0.5 * _mm(a_ref[...], wd_ref[...])
    x1_ref[...] = x1
    h2_ref[...] = _rms(x1, gm_ref[...]).astype(BF16)


def _outproj_ffn2_kernel(x1_ref, mix_ref, wout_ref, g2_ref, wgu_ref, wd_ref, gf_ref,
                         out_ref, h_ref, a_ref, *, final_norm):
    x2 = x1_ref[...] + _mm(mix_ref[...], wout_ref[...])
    h_ref[...] = _rms(x2, g2_ref[...]).astype(BF16)
    y = x2 + 0.5 * _swiglu(h_ref, wgu_ref, wd_ref, a_ref)
    out_ref[...] = _rms(y, gf_ref[...]) if final_norm else y


def _mixer_kernel(u_ref, vln_ref, q_ref, k_ref, kb_ref, vb_ref, zs_ref, bg_ref, dnn_ref,
                  wpair_ref, sgbias_ref, const_ref, mix_ref,
                  state_ref, l_ref, n_ref, nb_ref, z_ref, kt_ref, qk_ref, uwy_ref, w_ref, *, tt):
    n_chunks = tt // CHUNK
    units = [(n, hd) for n in range(n_chunks) for hd in range(DN_HEADS)]

    @pl.when(pl.program_id(1) == 0)
    def _():
        state_ref[...] = jnp.zeros_like(state_ref)

    lane = lax.broadcasted_iota(jnp.int32, (1, LANES), 1)
    lo = lane < (LANES // 2)
    zero = jnp.zeros((), BF16)
    for n in range(tt // SG_CHUNK):
        rows = slice(n * SG_CHUNK, (n + 1) * SG_CHUNK)
        for p in range(SG_GROUPS // 2):
            cols = slice(p * LANES, (p + 1) * LANES)
            vp = vln_ref[rows, cols]
            rhs = jnp.concatenate([jnp.where(lo, vp, zero), jnp.where(lo, zero, vp)], axis=0)
            vs = _mm(wpair_ref[p], rhs) + sgbias_ref[:, cols]
            mix_ref[rows, cols] = (u_ref[rows, cols].astype(F32) * vs).astype(BF16)

    strict = const_ref[IDX_STRICT]
    incl = const_ref[IDX_INCL]
    eye = const_ref[IDX_EYE]

    gcs, gcts = [], []
    for n in range(n_chunks):
        gc_all = _mm_f32(incl, bg_ref[n * CHUNK:(n + 1) * CHUNK, :])
        gcs.append(gc_all)
        gcts.append(gc_all.T)

    def cols_of(n, hd):
        return slice(n * CHUNK, (n + 1) * CHUNK), slice(hd * DN_HEAD_DIM, (hd + 1) * DN_HEAD_DIM)

    for ui, (n, hd) in enumerate(units):
        rows, hc = cols_of(n, hd)
        gc = gcs[n][:, DN_HEADS + hd:DN_HEADS + hd + 1]
        gc_row = gcts[n][DN_HEADS + hd:DN_HEADS + hd + 1, :]
        decay = jnp.exp(jnp.minimum(gc - gc_row, 0.0))
        kt = k_ref[rows, hc].T
        kt_ref[ui] = kt
        a = _mm(jnp.concatenate([kb_ref[rows, hc], q_ref[rows, hc]], axis=0), kt)
        l_mat = a[:CHUNK] * decay * strict
        l_ref[ui] = l_mat
        qk_ref[ui] = (a[CHUNK:] * decay * incl).astype(BF16)
        n0 = -(l_mat * const_ref[0])
        n_ref[ui] = n0
        nb_ref[ui] = n0.astype(BF16)

    for lvl in range(1, N_LEVELS):
        mask = const_ref[lvl]
        for ui in range(len(units)):
            y = l_ref[ui] * mask
            z_ref[ui] = y + _mm(nb_ref[ui], y.astype(BF16))
        for ui in range(len(units)):
            z = z_ref[ui]
            nn = n_ref[ui] - z - _mm(z.astype(BF16), nb_ref[ui])
            n_ref[ui] = nn
            nb_ref[ui] = nn.astype(BF16)

    for ui, (n, hd) in enumerate(units):
        rows, hc = cols_of(n, hd)
        t_mat = n_ref[ui] + eye
        e_row = jnp.exp(gcts[n][DN_HEADS + hd:DN_HEADS + hd + 1, :])
        uwy_ref[ui] = _mm(t_mat.astype(BF16), vb_ref[rows, hc])
        w_ref[ui] = _mm((t_mat * e_row).astype(BF16), kb_ref[rows, hc]).astype(BF16)

    for n in range(n_chunks):
        for hd in range(DN_HEADS):
            ui = n * DN_HEADS + hd
            rows, hc = cols_of(n, hd)
            gc = gcs[n][:, DN_HEADS + hd:DN_HEADS + hd + 1]
            gc_last = gcs[n][CHUNK - 1:CHUNK, DN_HEADS + hd:DN_HEADS + hd + 1]
            s = state_ref[hd]
            p = _mm(jnp.concatenate([w_ref[ui], q_ref[rows, hc]], axis=0), s.astype(BF16))
            v_new = uwy_ref[ui] - p[:CHUNK]
            o = p[CHUNK:] * jnp.exp(gc) + _mm(qk_ref[ui], v_new.astype(BF16))
            k_dec_v = (v_new * jnp.exp(gc_last - gc)).astype(BF16)
            state_ref[hd] = s * jnp.exp(gc_last) + _mm(kt_ref[ui], k_dec_v)
            o = o * lax.rsqrt(jnp.mean(o * o, axis=-1, keepdims=True) + EPS) * dnn_ref[...]
            mix_ref[rows, SG_WIDTH + hd * DN_HEAD_DIM:SG_WIDTH + (hd + 1) * DN_HEAD_DIM] = (
                o * zs_ref[rows, hc].astype(F32)).astype(BF16)


def _resident(shape):
    nd = len(shape)
    return pl.BlockSpec(shape, lambda *_: (0,) * nd, pipeline_mode=pl.Buffered(1))


def _prep_gate_up(w_gate, w_up):
    d, f = w_gate.shape
    nc = f // FF_CHUNK
    wg = w_gate.reshape(d, nc, FF_CHUNK)
    wu = w_up.reshape(d, nc, FF_CHUNK)
    return jnp.concatenate([wg, wu], axis=-1).transpose(1, 0, 2).astype(BF16)


def _mask_pack():
    i = jnp.arange(CHUNK)[:, None]
    j = jnp.arange(CHUNK)[None, :]
    rows = []
    for lvl in range(N_LEVELS):
        b = 1 << lvl
        rows.append((i // (2 * b) == j // (2 * b)) & ((i // b) % 2 == 1) & ((j // b) % 2 == 0))
    rows.append(i > j)
    rows.append(i >= j)
    rows.append(i == j)
    return jnp.stack(rows).astype(F32)


def _ffn1_inproj(x2d, params, tm, t_len):
    m, d = x2d.shape
    n_tiles = m // tm
    cur = lambda w: pl.BlockSpec((tm, w), lambda i: (jnp.minimum(i, n_tiles - 1), 0))
    prev = lambda w: pl.BlockSpec((tm, w), lambda i: (jnp.maximum(i - 1, 0), 0))
    bf = lambda w: jax.ShapeDtypeStruct((m, w), BF16)
    wd = params[2]
    return pl.pallas_call(
        functools.partial(_ffn1_inproj_kernel, tiles_per_seq=t_len // tm),
        grid=(n_tiles + 1,),
        in_specs=[cur(d)] + [_resident(p.shape) for p in params],
        out_specs=[cur(d), prev(SG_WIDTH), prev(SG_WIDTH)] + [prev(DN_WIDTH)] * 5 + [prev(LANES)],
        out_shape=[jax.ShapeDtypeStruct((m, d), F32), bf(SG_WIDTH), bf(SG_WIDTH)] + [bf(DN_WIDTH)] * 5
                  + [jax.ShapeDtypeStruct((m, LANES), F32)],
        scratch_shapes=[pltpu.VMEM((tm, d), BF16), pltpu.VMEM((tm, d), BF16), pltpu.VMEM((tm, wd.shape[0]), BF16),
                        pltpu.VMEM((tm + CONV_PAD, 3 * DN_WIDTH), F32),
                        pltpu.VMEM((tm, 3 * SG_WIDTH + LANES), F32)],
        compiler_params=pltpu.CompilerParams(dimension_semantics=("arbitrary",), vmem_limit_bytes=VMEM_LIMIT),
        name="ffn1_inproj",
    )(x2d, *params)


def _outproj_ffn2(x1, mix, wout, g2, wgu, wd, gf, tm, final_norm):
    m, d = x1.shape
    tok = lambda w: pl.BlockSpec((tm, w), lambda i: (i, 0))
    return pl.pallas_call(
        functools.partial(_outproj_ffn2_kernel, final_norm=final_norm),
        grid=(m // tm,),
        in_specs=[tok(d), tok(mix.shape[1]), _resident(wout.shape), _resident(g2.shape),
                  _resident(wgu.shape), _resident(wd.shape), _resident(gf.shape)],
        out_specs=tok(d),
        out_shape=jax.ShapeDtypeStruct((m, d), F32),
        scratch_shapes=[pltpu.VMEM((tm, d), BF16), pltpu.VMEM((tm, wd.shape[0]), BF16)],
        compiler_params=pltpu.CompilerParams(dimension_semantics=("arbitrary",), vmem_limit_bytes=VMEM_LIMIT),
        name="outproj_ffn2",
    )(x1, mix, wout, g2, wgu, wd, gf)


def _mixer(acts, params, bsz, t_len, tt):
    m = acts[0].shape[0]
    nt = t_len // tt
    n_units = (tt // CHUNK) * DN_HEADS
    tok = lambda w: pl.BlockSpec((tt, w), lambda b, t: (b * nt + t, 0))
    mat = lambda dt: pltpu.VMEM((n_units, CHUNK, CHUNK), dt)
    return pl.pallas_call(
        functools.partial(_mixer_kernel, tt=tt),
        grid=(bsz, nt),
        in_specs=[tok(a.shape[1]) for a in acts] + [_resident(p.shape) for p in params],
        out_specs=tok(SG_WIDTH + DN_WIDTH),
        out_shape=jax.ShapeDtypeStruct((m, SG_WIDTH + DN_WIDTH), BF16),
        scratch_shapes=[pltpu.VMEM((DN_HEADS, DN_HEAD_DIM, DN_HEAD_DIM), F32),
                        mat(F32), mat(F32), mat(BF16), mat(F32), mat(BF16), mat(BF16), mat(F32), mat(BF16)],
        compiler_params=pltpu.CompilerParams(dimension_semantics=("arbitrary", "arbitrary"),
                                             vmem_limit_bytes=VMEM_LIMIT),
        name="mixer",
    )(*acts, *params)


def _pad_lanes(v, offset):
    return jnp.zeros((1, LANES), F32).at[0, offset:offset + v.shape[0]].set(v.astype(F32))


def kernel(x, ffn1_norm, ffn1_w_gate, ffn1_w_up, ffn1_w_down, mix_norm, w_in, conv_w, a_log, dt_bias, dn_norm,
           sg_ln_g, sg_ln_b, sg_w, sg_b, w_out, ffn2_norm, ffn2_w_gate, ffn2_w_up, ffn2_w_down, final_norm):
    bsz, t_len, d = x.shape
    depth = ffn1_norm.shape[0]
    m = bsz * t_len
    tm = math.gcd(TOKEN_TILE, t_len)
    tt = math.gcd(TIME_TILE, t_len)
    assert t_len % CHUNK == 0 and SG_CHUNK == CHUNK and sg_w.shape[-1] == SG_CHUNK
    n_main = 2 * SG_WIDTH + 4 * DN_WIDTH
    assert w_in.shape[-1] == n_main + 2 * DN_HEADS

    consts = _mask_pack()
    causal = consts[IDX_INCL]
    row = lambda v: v.astype(F32).reshape(1, -1)
    xf = x.reshape(m, d)
    for l in range(depth):
        wsm = jnp.zeros((d, LANES), F32).at[:, :2 * DN_HEADS].set(w_in[l][:, n_main:]).astype(BF16)
        wc = sg_w[l] * causal
        wpair = jnp.concatenate([wc[0::2], wc[1::2]], axis=-1).astype(BF16)
        sgbias = jnp.repeat(sg_b[l].T.astype(F32), LANES // 2, axis=1)
        outs = _ffn1_inproj(
            xf, (row(ffn1_norm[l]), _prep_gate_up(ffn1_w_gate[l], ffn1_w_up[l]), ffn1_w_down[l].astype(BF16),
                 row(mix_norm[l]), w_in[l][:, :n_main].astype(BF16), wsm, conv_w[l].astype(F32),
                 _pad_lanes(a_log[l], DN_HEADS), _pad_lanes(dt_bias[l], DN_HEADS),
                 row(sg_ln_g[l]), row(sg_ln_b[l])), tm, t_len)
        x1, acts = outs[0], outs[1:]
        mix = _mixer(acts, (row(dn_norm[l]), wpair, sgbias, consts), bsz, t_len, tt)
        last = l == depth - 1
        xf = _outproj_ffn2(
            x1, mix, w_out[l].astype(BF16), row(ffn2_norm[l]),
            _prep_gate_up(ffn2_w_gate[l], ffn2_w_up[l]), ffn2_w_down[l].astype(BF16),
            row(final_norm) if last else row(ffn2_norm[l]), tm, last)
    return xf.reshape(bsz, t_len, d)
```

```python
import functools
import math

import jax
import jax.numpy as jnp
from jax import lax
from jax.experimental import pallas as pl
from jax.experimental.pallas import tpu as pltpu

F32 = jnp.float32
BF16 = jnp.bfloat16
EPS = 1e-6

LANES = 128
SG_GROUPS = 8
SG_CHUNK = 128
SG_WIDTH = SG_GROUPS * (LANES // 2)
DN_HEADS = 4
DN_HEAD_DIM = 128
DN_WIDTH = DN_HEADS * DN_HEAD_DIM
CONV_K = 4
CHUNK = 128
CONV_PAD = 8
FF_CHUNK = 256
TOKEN_TILE = 512
ROW_BLOCK = 64
TIME_TILE = 1024
BF16_ROWS = 16
VMEM_LIMIT = 58 * 1024 * 1024

N_LEVELS = 7
IDX_STRICT = N_LEVELS
IDX_INCL = N_LEVELS + 1
IDX_EYE = N_LEVELS + 2
N_CONST = N_LEVELS + 3


def _mm(a, b):
    return jnp.dot(a, b, preferred_element_type=F32)


def _rms(x, g):
    return x * lax.rsqrt(jnp.mean(x * x, axis=-1, keepdims=True) + EPS) * g


def _silu(x):
    return x * jax.nn.sigmoid(x)


def _gelu(x):
    return jax.nn.gelu(x, approximate=True)


def _swiglu_chunk(h, wgu_ref, a_ref, c):
    cols = slice(c * FF_CHUNK, (c + 1) * FF_CHUNK)
    a_ref[:, cols] = (_silu(_mm(h, wgu_ref[0, :, cols])) * _mm(h, wgu_ref[1, :, cols])).astype(BF16)


def _swiglu(h_ref, wgu_ref, wd_ref, a_ref):
    h = h_ref[...]
    for c in range(wgu_ref.shape[2] // FF_CHUNK):
        _swiglu_chunk(h, wgu_ref, a_ref, c)
    return _mm(a_ref[...], wd_ref[...])


def _ffn1_inproj_kernel(x_ref, g1_ref, wgu_ref, wd_ref, gm_ref, win_ref, wsm_ref, convw_ref, alog_ref, dtb_ref,
                        lng_ref, lnb_ref,
                        x1_ref, u_ref, vln_ref, q_ref, k_ref, kb_ref, vb_ref, zs_ref, bg_ref,
                        h_ref, h2_ref, a_ref, cbuf_ref, st_ref, *, tiles_per_seq):
    tm = x_ref.shape[0]
    step = pl.program_id(0)

    @pl.when(step == 0)
    def _():
        h2_ref[...] = jnp.zeros_like(h2_ref)
        cbuf_ref[0:CONV_PAD, :] = jnp.zeros((CONV_PAD, cbuf_ref.shape[1]), F32)

    seq_start = (step - 1) % tiles_per_seq == 0
    cbuf_ref[0:CONV_PAD, :] = jnp.where(seq_start, 0.0, cbuf_ref[0:CONV_PAD, :])

    x = x_ref[...]
    h_ref[...] = _rms(x, g1_ref[...]).astype(BF16)
    h1 = h_ref[...]
    h2 = h2_ref[...]
    row_blocks = [slice(rb * ROW_BLOCK, (rb + 1) * ROW_BLOCK) for rb in range(tm // ROW_BLOCK)]
    lane = lax.broadcasted_iota(jnp.int32, (1, LANES), 1)
    scale = DN_HEAD_DIM ** -0.5

    def gates():
        st_ref[:, 3 * SG_WIDTH:] = _mm(h2, wsm_ref[...])
        for rows in row_blocks:
            sm = st_ref[rows, 3 * SG_WIDTH:]
            xs = sm + dtb_ref[...]
            softplus = jnp.maximum(xs, 0.0) + jnp.log(1.0 + jnp.exp(-jnp.abs(xs)))
            bg_ref[rows, :] = jnp.where(lane < DN_HEADS, jax.nn.sigmoid(sm), -jnp.exp(alog_ref[...]) * softplus)

    def conv_silu(rb, cols):
        r0 = rb * ROW_BLOCK
        xin = cbuf_ref[r0:r0 + CONV_PAD + ROW_BLOCK, cols]
        acc = convw_ref[CONV_K - 1:CONV_K, cols] * xin
        for s in range(1, CONV_K):
            acc = acc + convw_ref[CONV_K - 1 - s:CONV_K - s, cols] * pltpu.roll(xin, s, 0)
        return _silu(acc[CONV_PAD:])

    def dn_part(part):
        src = 2 * SG_WIDTH + part * DN_WIDTH
        cbuf_ref[CONV_PAD:CONV_PAD + tm, part * DN_WIDTH:(part + 1) * DN_WIDTH] = _mm(
            h2, win_ref[:, src:src + DN_WIDTH])
        for rb, rows in enumerate(row_blocks):
            for hd in range(DN_HEADS):
                hc = slice(hd * DN_HEAD_DIM, (hd + 1) * DN_HEAD_DIM)
                y = conv_silu(rb, slice(part * DN_WIDTH + hd * DN_HEAD_DIM, part * DN_WIDTH + (hd + 1) * DN_HEAD_DIM))
                if part == 0:
                    q_ref[rows, hc] = (y * (lax.rsqrt(jnp.sum(y * y, axis=-1, keepdims=True) + EPS) * scale)
                                       ).astype(BF16)
                elif part == 1:
                    k = y * lax.rsqrt(jnp.sum(y * y, axis=-1, keepdims=True) + EPS)
                    k_ref[rows, hc] = k.astype(BF16)
                    kb_ref[rows, hc] = (k * bg_ref[rows, hd:hd + 1]).astype(BF16)
                else:
                    vb_ref[rows, hc] = (y * bg_ref[rows, hd:hd + 1]).astype(BF16)

    def sg_u():
        st_ref[:, 0:SG_WIDTH] = _mm(h2, win_ref[:, 0:SG_WIDTH])
        for rows in row_blocks:
            u_ref[rows, :] = _gelu(st_ref[rows, 0:SG_WIDTH]).astype(BF16)

    def sg_v():
        st_ref[:, SG_WIDTH:2 * SG_WIDTH] = _mm(h2, win_ref[:, SG_WIDTH:2 * SG_WIDTH])
        for rows in row_blocks:
            v = _gelu(st_ref[rows, SG_WIDTH:2 * SG_WIDTH])
            vc = v - jnp.mean(v, axis=-1, keepdims=True)
            var = jnp.mean(vc * vc, axis=-1, keepdims=True)
            vln_ref[rows, :] = (vc * lax.rsqrt(var + EPS) * lng_ref[...] + lnb_ref[...]).astype(BF16)

    def dn_z():
        src = 2 * SG_WIDTH + 3 * DN_WIDTH
        st_ref[:, 2 * SG_WIDTH:3 * SG_WIDTH] = _mm(h2, win_ref[:, src:src + DN_WIDTH])
        for rows in row_blocks:
            zs_ref[rows, :] = _silu(st_ref[rows, 2 * SG_WIDTH:3 * SG_WIDTH]).astype(BF16)

    tasks = [gates, functools.partial(dn_part, 0), functools.partial(dn_part, 1), functools.partial(dn_part, 2),
             sg_u, sg_v, dn_z]
    n_ff = wgu_ref.shape[2] // FF_CHUNK
    for t, task in enumerate(tasks):
        task()
        for c in range(n_ff):
            if c * len(tasks) // n_ff == t:
                _swiglu_chunk(h1, wgu_ref, a_ref, c)
    cbuf_ref[0:CONV_PAD, :] = cbuf_ref[tm:tm + CONV_PAD, :]

    x1 = x + 0.5 * _mm(a_ref[...], wd_ref[...])
    x1_ref[...] = x1
    h2_ref[...] = _rms(x1, gm_ref[...]).astype(BF16)


def _outproj_ffn2_kernel(x1_ref, mix_ref, wout_ref, g2_ref, wgu_ref, wd_ref, gf_ref,
                         out_ref, h_ref, a_ref, *, final_norm):
    x2 = x1_ref[...] + _mm(mix_ref[...], wout_ref[...])
    h_ref[...] = _rms(x2, g2_ref[...]).astype(BF16)
    y = x2 + 0.5 * _swiglu(h_ref, wgu_ref, wd_ref, a_ref)
    out_ref[...] = _rms(y, gf_ref[...]) if final_norm else y


def _mixer_kernel(u_ref, vln_ref, q_ref, k_ref, kb_ref, vb_ref, zs_ref, bg_ref, dnn_ref,
                  wpair_ref, sgbias_ref, const_ref, mix_ref,
                  state_ref, l_ref, n_ref, nb_ref, z_ref, kt_ref, qk_ref, uwy_ref, w_ref, *, tt):
    n_chunks = tt // CHUNK
    units = [(n, hd) for n in range(n_chunks) for hd in range(DN_HEADS)]

    @pl.when(pl.program_id(1) == 0)
    def _():
        state_ref[...] = jnp.zeros_like(state_ref)

    lane = lax.broadcasted_iota(jnp.int32, (1, LANES), 1)
    lo = lane < (LANES // 2)
    zero = jnp.zeros((), BF16)
    for n in range(tt // SG_CHUNK):
        rows = slice(n * SG_CHUNK, (n + 1) * SG_CHUNK)
        for p in range(SG_GROUPS // 2):
            cols = slice(p * LANES, (p + 1) * LANES)
            vp = vln_ref[rows, cols]
            rhs = jnp.concatenate([jnp.where(lo, vp, zero), jnp.where(lo, zero, vp)], axis=0)
            vs = _mm(wpair_ref[p], rhs) + sgbias_ref[:, cols]
            mix_ref[rows, cols] = (u_ref[rows, cols].astype(F32) * vs).astype(BF16)

    strict = const_ref[IDX_STRICT]
    incl = const_ref[IDX_INCL]
    eye = const_ref[IDX_EYE]

    gcs, gcts = [], []
    row_id = lax.broadcasted_iota(jnp.int32, (CHUNK, LANES), 0)
    for n in range(n_chunks):
        gc_all = bg_ref[n * CHUNK:(n + 1) * CHUNK, :]
        shift = 1
        while shift < CHUNK:
            gc_all = gc_all + jnp.where(row_id >= shift, pltpu.roll(gc_all, shift, 0), 0.0)
            shift *= 2
        gcs.append(gc_all)
        gcts.append(gc_all.T)

    def cols_of(n, hd):
        return slice(n * CHUNK, (n + 1) * CHUNK), slice(hd * DN_HEAD_DIM, (hd + 1) * DN_HEAD_DIM)

    for ui, (n, hd) in enumerate(units):
        rows, hc = cols_of(n, hd)
        gc = gcs[n][:, DN_HEADS + hd:DN_HEADS + hd + 1]
        gc_row = gcts[n][DN_HEADS + hd:DN_HEADS + hd + 1, :]
        decay = jnp.exp(jnp.minimum(gc - gc_row, 0.0))
        kt = k_ref[rows, hc].T
        kt_ref[ui] = kt
        a = _mm(jnp.concatenate([kb_ref[rows, hc], q_ref[rows, hc]], axis=0), kt)
        l_mat = a[:CHUNK] * decay * strict
        l_ref[ui] = l_mat
        qk_ref[ui] = (a[CHUNK:] * decay * incl).astype(BF16)
        n0 = -(l_mat * const_ref[0])
        n_ref[ui] = n0
        nb_ref[ui] = n0.astype(BF16)

    for lvl in range(1, N_LEVELS):
        blk = 1 << lvl
        mask = const_ref[lvl]
        if blk < BF16_ROWS:
            for ui in range(len(units)):
                y = l_ref[ui] * mask
                z_ref[ui] = y + _mm(nb_ref[ui], y.astype(BF16))
            for ui in range(len(units)):
                z = z_ref[ui]
                nn = n_ref[ui] - z - _mm(z.astype(BF16), nb_ref[ui])
                n_ref[ui] = nn
                nb_ref[ui] = nn.astype(BF16)
        else:
            starts = range(blk, CHUNK, 2 * blk)
            half = CHUNK // 2

            def lower_rows(ref, ui):
                return jnp.concatenate([ref[ui, r:r + blk, :] for r in starts], axis=0)

            for ui in range(len(units)):
                y = l_ref[ui] * mask
                y_low = jnp.concatenate([y[r:r + blk] for r in starts], axis=0)
                z_ref[ui, 0:half, :] = y_low + _mm(lower_rows(nb_ref, ui), y.astype(BF16))
            for ui in range(len(units)):
                z = z_ref[ui, 0:half, :]
                nn = lower_rows(n_ref, ui) - z - _mm(z.astype(BF16), nb_ref[ui])
                for i, r in enumerate(starts):
                    n_ref[ui, r:r + blk, :] = nn[i * blk:(i + 1) * blk]
                    nb_ref[ui, r:r + blk, :] = nn[i * blk:(i + 1) * blk].astype(BF16)

    for ui, (n, hd) in enumerate(units):
        rows, hc = cols_of(n, hd)
        t_mat = n_ref[ui] + eye
        e_row = jnp.exp(gcts[n][DN_HEADS + hd:DN_HEADS + hd + 1, :])
        uwy_ref[ui] = _mm(t_mat.astype(BF16), vb_ref[rows, hc])
        w_ref[ui] = _mm((t_mat * e_row).astype(BF16), kb_ref[rows, hc]).astype(BF16)

    for n in range(n_chunks):
        for hd in range(DN_HEADS):
            ui = n * DN_HEADS + hd
            rows, hc = cols_of(n, hd)
            gc = gcs[n][:, DN_HEADS + hd:DN_HEADS + hd + 1]
            gc_last = gcs[n][CHUNK - 1:CHUNK, DN_HEADS + hd:DN_HEADS + hd + 1]
            s = state_ref[hd]
            p = _mm(jnp.concatenate([w_ref[ui], q_ref[rows, hc]], axis=0), s.astype(BF16))
            v_new = uwy_ref[ui] - p[:CHUNK]
            o = p[CHUNK:] * jnp.exp(gc) + _mm(qk_ref[ui], v_new.astype(BF16))
            k_dec_v = (v_new * jnp.exp(gc_last - gc)).astype(BF16)
            state_ref[hd] = s * jnp.exp(gc_last) + _mm(kt_ref[ui], k_dec_v)
            o = o * lax.rsqrt(jnp.mean(o * o, axis=-1, keepdims=True) + EPS) * dnn_ref[...]
            mix_ref[rows, SG_WIDTH + hd * DN_HEAD_DIM:SG_WIDTH + (hd + 1) * DN_HEAD_DIM] = (
                o * zs_ref[rows, hc].astype(F32)).astype(BF16)


def _resident(shape):
    nd = len(shape)
    return pl.BlockSpec(shape, lambda *_: (0,) * nd, pipeline_mode=pl.Buffered(1))


def _prep_gate_up(w_gate, w_up):
    return jnp.stack([w_gate, w_up]).astype(BF16)


def _mask_pack():
    i = jnp.arange(CHUNK)[:, None]
    j = jnp.arange(CHUNK)[None, :]
    rows = []
    for lvl in range(N_LEVELS):
        b = 1 << lvl
        rows.append((i // (2 * b) == j // (2 * b)) & ((i // b) % 2 == 1) & ((j // b) % 2 == 0))
    rows.append(i > j)
    rows.append(i >= j)
    rows.append(i == j)
    return jnp.stack(rows).astype(F32)


def _ffn1_inproj(x2d, params, tm, t_len):
    m, d = x2d.shape
    n_tiles = m // tm
    cur = lambda w: pl.BlockSpec((tm, w), lambda i: (jnp.minimum(i, n_tiles - 1), 0))
    prev = lambda w: pl.BlockSpec((tm, w), lambda i: (jnp.maximum(i - 1, 0), 0))
    bf = lambda w: jax.ShapeDtypeStruct((m, w), BF16)
    wd = params[2]
    return pl.pallas_call(
        functools.partial(_ffn1_inproj_kernel, tiles_per_seq=t_len // tm),
        grid=(n_tiles + 1,),
        in_specs=[cur(d)] + [_resident(p.shape) for p in params],
        out_specs=[cur(d), prev(SG_WIDTH), prev(SG_WIDTH)] + [prev(DN_WIDTH)] * 5 + [prev(LANES)],
        out_shape=[jax.ShapeDtypeStruct((m, d), F32), bf(SG_WIDTH), bf(SG_WIDTH)] + [bf(DN_WIDTH)] * 5
                  + [jax.ShapeDtypeStruct((m, LANES), F32)],
        scratch_shapes=[pltpu.VMEM((tm, d), BF16), pltpu.VMEM((tm, d), BF16), pltpu.VMEM((tm, wd.shape[0]), BF16),
                        pltpu.VMEM((tm + CONV_PAD, 3 * DN_WIDTH), F32),
                        pltpu.VMEM((tm, 3 * SG_WIDTH + LANES), F32)],
        compiler_params=pltpu.CompilerParams(dimension_semantics=("arbitrary",), vmem_limit_bytes=VMEM_LIMIT),
        name="ffn1_inproj",
    )(x2d, *params)


def _outproj_ffn2(x1, mix, wout, g2, wgu, wd, gf, tm, final_norm):
    m, d = x1.shape
    tok = lambda w: pl.BlockSpec((tm, w), lambda i: (i, 0))
    return pl.pallas_call(
        functools.partial(_outproj_ffn2_kernel, final_norm=final_norm),
        grid=(m // tm,),
        in_specs=[tok(d), tok(mix.shape[1]), _resident(wout.shape), _resident(g2.shape),
                  _resident(wgu.shape), _resident(wd.shape), _resident(gf.shape)],
        out_specs=tok(d),
        out_shape=jax.ShapeDtypeStruct((m, d), F32),
        scratch_shapes=[pltpu.VMEM((tm, d), BF16), pltpu.VMEM((tm, wd.shape[0]), BF16)],
        compiler_params=pltpu.CompilerParams(dimension_semantics=("arbitrary",), vmem_limit_bytes=VMEM_LIMIT),
        name="outproj_ffn2",
    )(x1, mix, wout, g2, wgu, wd, gf)


def _mixer(acts, params, bsz, t_len, tt):
    m = acts[0].shape[0]
    nt = t_len // tt
    n_units = (tt // CHUNK) * DN_HEADS
    tok = lambda w: pl.BlockSpec((tt, w), lambda b, t: (b * nt + t, 0))
    mat = lambda dt: pltpu.VMEM((n_units, CHUNK, CHUNK), dt)
    return pl.pallas_call(
        functools.partial(_mixer_kernel, tt=tt),
        grid=(bsz, nt),
        in_specs=[tok(a.shape[1]) for a in acts] + [_resident(p.shape) for p in params],
        out_specs=tok(SG_WIDTH + DN_WIDTH),
        out_shape=jax.ShapeDtypeStruct((m, SG_WIDTH + DN_WIDTH), BF16),
        scratch_shapes=[pltpu.VMEM((DN_HEADS, DN_HEAD_DIM, DN_HEAD_DIM), F32),
                        mat(F32), mat(F32), mat(BF16), mat(F32), mat(BF16), mat(BF16), mat(F32), mat(BF16)],
        compiler_params=pltpu.CompilerParams(dimension_semantics=("arbitrary", "arbitrary"),
                                             vmem_limit_bytes=VMEM_LIMIT),
        name="mixer",
    )(*acts, *params)


def _pad_lanes(v, offset):
    return jnp.zeros((1, LANES), F32).at[0, offset:offset + v.shape[0]].set(v.astype(F32))


def kernel(x, ffn1_norm, ffn1_w_gate, ffn1_w_up, ffn1_w_down, mix_norm, w_in, conv_w, a_log, dt_bias, dn_norm,
           sg_ln_g, sg_ln_b, sg_w, sg_b, w_out, ffn2_norm, ffn2_w_gate, ffn2_w_up, ffn2_w_down, final_norm):
    bsz, t_len, d = x.shape
    depth = ffn1_norm.shape[0]
    m = bsz * t_len
    tm = math.gcd(TOKEN_TILE, t_len)
    tt = math.gcd(TIME_TILE, t_len)
    assert t_len % CHUNK == 0 and SG_CHUNK == CHUNK and sg_w.shape[-1] == SG_CHUNK
    n_main = 2 * SG_WIDTH + 4 * DN_WIDTH
    assert w_in.shape[-1] == n_main + 2 * DN_HEADS

    consts = _mask_pack()
    causal = consts[IDX_INCL]
    row = lambda v: v.astype(F32).reshape(1, -1)
    xf = x.reshape(m, d)
    for l in range(depth):
        wsm = jnp.zeros((d, LANES), F32).at[:, :2 * DN_HEADS].set(w_in[l][:, n_main:]).astype(BF16)
        wc = sg_w[l] * causal
        wpair = jnp.concatenate([wc[0::2], wc[1::2]], axis=-1).astype(BF16)
        sgbias = jnp.repeat(sg_b[l].T.astype(F32), LANES // 2, axis=1)
        outs = _ffn1_inproj(
            xf, (row(ffn1_norm[l]), _prep_gate_up(ffn1_w_gate[l], ffn1_w_up[l]), ffn1_w_down[l].astype(BF16),
                 row(mix_norm[l]), w_in[l][:, :n_main].astype(BF16), wsm, conv_w[l].astype(F32),
                 _pad_lanes(a_log[l], DN_HEADS), _pad_lanes(dt_bias[l], DN_HEADS),
                 row(sg_ln_g[l]), row(sg_ln_b[l])), tm, t_len)
        x1, acts = outs[0], outs[1:]
        mix = _mixer(acts, (row(dn_norm[l]), wpair, sgbias, consts), bsz, t_len, tt)
        last = l == depth - 1
        xf = _outproj_ffn2(
            x1, mix, w_out[l].astype(BF16), row(ffn2_norm[l]),
            _prep_gate_up(ffn2_w_gate[l], ffn2_w_up[l]), ffn2_w_down[l].astype(BF16),
            row(final_norm) if last else row(ffn2_norm[l]), tm, last)
    return xf.reshape(bsz, t_len, d)
```

```python
import functools
import math

import jax
import jax.numpy as jnp
from jax import lax
from jax.experimental import pallas as pl
from jax.experimental.pallas import tpu as pltpu

F32 = jnp.float32
BF16 = jnp.bfloat16
EPS = 1e-6

LANES = 128
BF16_ROWS = 16
SG_GROUPS = 8
SG_CHUNK = 128
SG_WIDTH = SG_GROUPS * (LANES // 2)
DN_HEADS = 4
DN_HEAD_DIM = 128
DN_WIDTH = DN_HEADS * DN_HEAD_DIM
CONV_K = 4
CHUNK = 128
CONV_PAD = 8
FF_CHUNK = 256
TOKEN_TILE = 512
ROW_BLOCK = 64
TIME_TILE = 1024
VMEM_LIMIT = 58 * 1024 * 1024

N_LEVELS = 7
IDX_STRICT = N_LEVELS
IDX_INCL = N_LEVELS + 1
IDX_EYE = N_LEVELS + 2
N_CONST = N_LEVELS + 3


def _mm(a, b):
    return jnp.dot(a, b, preferred_element_type=F32)


def _rms(x, g):
    return x * lax.rsqrt(jnp.mean(x * x, axis=-1, keepdims=True) + EPS) * g


def _silu(x):
    return x * jax.nn.sigmoid(x)


def _gelu(x):
    return jax.nn.gelu(x, approximate=True)


def _swiglu_chunk(h, wgu_ref, a_ref, c):
    cols = slice(c * FF_CHUNK, (c + 1) * FF_CHUNK)
    a_ref[:, cols] = (_silu(_mm(h, wgu_ref[0, :, cols])) * _mm(h, wgu_ref[1, :, cols])).astype(BF16)


def _swiglu(h_ref, wgu_ref, wd_ref, a_ref):
    h = h_ref[...]
    for c in range(wgu_ref.shape[2] // FF_CHUNK):
        _swiglu_chunk(h, wgu_ref, a_ref, c)
    return _mm(a_ref[...], wd_ref[...])


def _ffn1_inproj_kernel(x_ref, g1_ref, wgu_ref, wd_ref, gm_ref, win_ref, wsm_ref, convw_ref, alog_ref, dtb_ref,
                        lng_ref, lnb_ref,
                        x1_ref, u_ref, vln_ref, q_ref, k_ref, kb_ref, vb_ref, zs_ref, bg_ref,
                        h_ref, h2_ref, a_ref, cbuf_ref, st_ref, *, tiles_per_seq):
    tm = x_ref.shape[0]
    step = pl.program_id(0)

    @pl.when(step == 0)
    def _():
        h2_ref[...] = jnp.zeros_like(h2_ref)
        cbuf_ref[0:CONV_PAD, :] = jnp.zeros((CONV_PAD, cbuf_ref.shape[1]), F32)

    seq_start = (step - 1) % tiles_per_seq == 0
    cbuf_ref[0:CONV_PAD, :] = jnp.where(seq_start, 0.0, cbuf_ref[0:CONV_PAD, :])

    x = x_ref[...]
    h_ref[...] = _rms(x, g1_ref[...]).astype(BF16)
    h1 = h_ref[...]
    h2 = h2_ref[...]
    row_blocks = [slice(rb * ROW_BLOCK, (rb + 1) * ROW_BLOCK) for rb in range(tm // ROW_BLOCK)]
    lane = lax.broadcasted_iota(jnp.int32, (1, LANES), 1)
    scale = DN_HEAD_DIM ** -0.5

    def gates():
        st_ref[:, 3 * SG_WIDTH:] = _mm(h2, wsm_ref[...])
        for rows in row_blocks:
            sm = st_ref[rows, 3 * SG_WIDTH:]
            xs = sm + dtb_ref[...]
            softplus = jnp.maximum(xs, 0.0) + jnp.log(1.0 + jnp.exp(-jnp.abs(xs)))
            bg_ref[rows, :] = jnp.where(lane < DN_HEADS, jax.nn.sigmoid(sm), -jnp.exp(alog_ref[...]) * softplus)

    def conv_silu(rb, cols):
        r0 = rb * ROW_BLOCK
        xin = cbuf_ref[r0:r0 + CONV_PAD + ROW_BLOCK, cols]
        acc = convw_ref[CONV_K - 1:CONV_K, cols] * xin
        for s in range(1, CONV_K):
            acc = acc + convw_ref[CONV_K - 1 - s:CONV_K - s, cols] * pltpu.roll(xin, s, 0)
        return _silu(acc[CONV_PAD:])

    def dn_part(part):
        src = 2 * SG_WIDTH + part * DN_WIDTH
        cbuf_ref[CONV_PAD:CONV_PAD + tm, part * DN_WIDTH:(part + 1) * DN_WIDTH] = _mm(
            h2, win_ref[:, src:src + DN_WIDTH])
        for rb, rows in enumerate(row_blocks):
            for hd in range(DN_HEADS):
                hc = slice(hd * DN_HEAD_DIM, (hd + 1) * DN_HEAD_DIM)
                y = conv_silu(rb, slice(part * DN_WIDTH + hd * DN_HEAD_DIM, part * DN_WIDTH + (hd + 1) * DN_HEAD_DIM))
                if part == 0:
                    q_ref[rows, hc] = (y * (lax.rsqrt(jnp.sum(y * y, axis=-1, keepdims=True) + EPS) * scale)
                                       ).astype(BF16)
                elif part == 1:
                    k = y * lax.rsqrt(jnp.sum(y * y, axis=-1, keepdims=True) + EPS)
                    k_ref[rows, hc] = k.astype(BF16)
                    kb_ref[rows, hc] = (k * bg_ref[rows, hd:hd + 1]).astype(BF16)
                else:
                    vb_ref[rows, hc] = (y * bg_ref[rows, hd:hd + 1]).astype(BF16)

    def sg_u():
        st_ref[:, 0:SG_WIDTH] = _mm(h2, win_ref[:, 0:SG_WIDTH])
        for rows in row_blocks:
            u_ref[rows, :] = _gelu(st_ref[rows, 0:SG_WIDTH]).astype(BF16)

    def sg_v():
        st_ref[:, SG_WIDTH:2 * SG_WIDTH] = _mm(h2, win_ref[:, SG_WIDTH:2 * SG_WIDTH])
        for rows in row_blocks:
            v = _gelu(st_ref[rows, SG_WIDTH:2 * SG_WIDTH])
            vc = v - jnp.mean(v, axis=-1, keepdims=True)
            var = jnp.mean(vc * vc, axis=-1, keepdims=True)
            vln_ref[rows, :] = (vc * lax.rsqrt(var + EPS) * lng_ref[...] + lnb_ref[...]).astype(BF16)

    def dn_z():
        src = 2 * SG_WIDTH + 3 * DN_WIDTH
        st_ref[:, 2 * SG_WIDTH:3 * SG_WIDTH] = _mm(h2, win_ref[:, src:src + DN_WIDTH])
        for rows in row_blocks:
            zs_ref[rows, :] = _silu(st_ref[rows, 2 * SG_WIDTH:3 * SG_WIDTH]).astype(BF16)

    tasks = [gates, functools.partial(dn_part, 0), functools.partial(dn_part, 1), functools.partial(dn_part, 2),
             sg_u, sg_v, dn_z]
    n_ff = wgu_ref.shape[2] // FF_CHUNK
    for t, task in enumerate(tasks):
        task()
        for c in range(n_ff):
            if c * len(tasks) // n_ff == t:
                _swiglu_chunk(h1, wgu_ref, a_ref, c)
    cbuf_ref[0:CONV_PAD, :] = cbuf_ref[tm:tm + CONV_PAD, :]

    x1 = x + 0.5 * _mm(a_ref[...], wd_ref[...])
    x1_ref[...] = x1
    h2_ref[...] = _rms(x1, gm_ref[...]).astype(BF16)


def _outproj_ffn2_kernel(x1_ref, mix_ref, wout_ref, g2_ref, wgu_ref, wd_ref, gf_ref,
                         out_ref, h_ref, a_ref, *, final_norm):
    x2 = x1_ref[...] + _mm(mix_ref[...], wout_ref[...])
    h_ref[...] = _rms(x2, g2_ref[...]).astype(BF16)
    y = x2 + 0.5 * _swiglu(h_ref, wgu_ref, wd_ref, a_ref)
    out_ref[...] = _rms(y, gf_ref[...]) if final_norm else y


def _mixer_kernel(u_ref, vln_ref, q_ref, k_ref, kb_ref, vb_ref, zs_ref, bg_ref, dnn_ref,
                  wpair_ref, sgbias_ref, const_ref, constb_ref, mix_ref,
                  state_ref, lb_ref, tb_ref, zb_ref, kt_ref, qk_ref, uwy_ref, w_ref, *, tt):
    n_chunks = tt // CHUNK
    units = [(n, hd) for n in range(n_chunks) for hd in range(DN_HEADS)]

    @pl.when(pl.program_id(1) == 0)
    def _():
        state_ref[...] = jnp.zeros_like(state_ref)

    lane = lax.broadcasted_iota(jnp.int32, (1, LANES), 1)
    lo = lane < (LANES // 2)
    zero = jnp.zeros((), BF16)
    for n in range(tt // SG_CHUNK):
        rows = slice(n * SG_CHUNK, (n + 1) * SG_CHUNK)
        for p in range(SG_GROUPS // 2):
            cols = slice(p * LANES, (p + 1) * LANES)
            vp = vln_ref[rows, cols]
            rhs = jnp.concatenate([jnp.where(lo, vp, zero), jnp.where(lo, zero, vp)], axis=0)
            vs = _mm(wpair_ref[p], rhs) + sgbias_ref[:, cols]
            mix_ref[rows, cols] = (u_ref[rows, cols].astype(F32) * vs).astype(BF16)

    strict = const_ref[IDX_STRICT]
    incl = const_ref[IDX_INCL]

    gcs, gcts = [], []
    row_id = lax.broadcasted_iota(jnp.int32, (CHUNK, LANES), 0)
    for n in range(n_chunks):
        gc_all = bg_ref[n * CHUNK:(n + 1) * CHUNK, :]
        shift = 1
        while shift < CHUNK:
            gc_all = gc_all + jnp.where(row_id >= shift, pltpu.roll(gc_all, shift, 0), 0.0)
            shift *= 2
        gcs.append(gc_all)
        gcts.append(gc_all.T)

    def cols_of(n, hd):
        return slice(n * CHUNK, (n + 1) * CHUNK), slice(hd * DN_HEAD_DIM, (hd + 1) * DN_HEAD_DIM)

    for ui, (n, hd) in enumerate(units):
        rows, hc = cols_of(n, hd)
        gc = gcs[n][:, DN_HEADS + hd:DN_HEADS + hd + 1]
        gc_row = gcts[n][DN_HEADS + hd:DN_HEADS + hd + 1, :]
        decay = jnp.exp(jnp.minimum(gc - gc_row, 0.0))
        kt = k_ref[rows, hc].T
        kt_ref[ui] = kt
        a = _mm(jnp.concatenate([kb_ref[rows, hc], q_ref[rows, hc]], axis=0), kt)
        lb = (a[:CHUNK] * decay * strict).astype(BF16)
        lb_ref[ui] = lb
        qk_ref[ui] = (a[CHUNK:] * decay * incl).astype(BF16)
        tb_ref[ui] = constb_ref[IDX_EYE] + lb * constb_ref[0]

    for lvl in range(1, N_LEVELS):
        blk = 1 << lvl
        neg_mask = constb_ref[lvl]
        if blk < BF16_ROWS:
            for ui in range(len(units)):
                zb_ref[ui] = _mm(tb_ref[ui], lb_ref[ui] * neg_mask).astype(BF16)
            for ui in range(len(units)):
                tb_ref[ui] = tb_ref[ui] + _mm(zb_ref[ui], tb_ref[ui]).astype(BF16)
        else:
            starts = range(blk, CHUNK, 2 * blk)
            half = CHUNK // 2
            for ui in range(len(units)):
                t_low = jnp.concatenate([tb_ref[ui, r:r + blk, :] for r in starts], axis=0)
                zb_ref[ui, 0:half, :] = _mm(t_low, lb_ref[ui] * neg_mask).astype(BF16)
            for ui in range(len(units)):
                new = _mm(zb_ref[ui, 0:half, :], tb_ref[ui]).astype(BF16)
                for i, r in enumerate(starts):
                    tb_ref[ui, r:r + blk, :] = tb_ref[ui, r:r + blk, :] + new[i * blk:(i + 1) * blk]

    for ui, (n, hd) in enumerate(units):
        rows, hc = cols_of(n, hd)
        t_mat = tb_ref[ui]
        e_row = jnp.exp(gcts[n][DN_HEADS + hd:DN_HEADS + hd + 1, :])
        uwy_ref[ui] = _mm(t_mat, vb_ref[rows, hc])
        w_ref[ui] = _mm((t_mat.astype(F32) * e_row).astype(BF16), kb_ref[rows, hc]).astype(BF16)

    for n in range(n_chunks):
        for hd in range(DN_HEADS):
            ui = n * DN_HEADS + hd
            rows, hc = cols_of(n, hd)
            gc = gcs[n][:, DN_HEADS + hd:DN_HEADS + hd + 1]
            gc_last = gcs[n][CHUNK - 1:CHUNK, DN_HEADS + hd:DN_HEADS + hd + 1]
            s = state_ref[hd]
            p = _mm(jnp.concatenate([w_ref[ui], q_ref[rows, hc]], axis=0), s.astype(BF16))
            v_new = uwy_ref[ui] - p[:CHUNK]
            o = p[CHUNK:] * jnp.exp(gc) + _mm(qk_ref[ui], v_new.astype(BF16))
            k_dec_v = (v_new * jnp.exp(gc_last - gc)).astype(BF16)
            state_ref[hd] = s * jnp.exp(gc_last) + _mm(kt_ref[ui], k_dec_v)
            o = o * lax.rsqrt(jnp.mean(o * o, axis=-1, keepdims=True) + EPS) * dnn_ref[...]
            mix_ref[rows, SG_WIDTH + hd * DN_HEAD_DIM:SG_WIDTH + (hd + 1) * DN_HEAD_DIM] = (
                o * zs_ref[rows, hc].astype(F32)).astype(BF16)


def _resident(shape):
    nd = len(shape)
    return pl.BlockSpec(shape, lambda *_: (0,) * nd, pipeline_mode=pl.Buffered(1))


def _prep_gate_up(w_gate, w_up):
    return jnp.stack([w_gate, w_up]).astype(BF16)


def _mask_pack():
    i = jnp.arange(CHUNK)[:, None]
    j = jnp.arange(CHUNK)[None, :]
    rows = []
    for lvl in range(N_LEVELS):
        b = 1 << lvl
        rows.append((i // (2 * b) == j // (2 * b)) & ((i // b) % 2 == 1) & ((j // b) % 2 == 0))
    rows.append(i > j)
    rows.append(i >= j)
    rows.append(i == j)
    return jnp.stack(rows).astype(F32)


def _ffn1_inproj(x2d, params, tm, t_len):
    m, d = x2d.shape
    n_tiles = m // tm
    cur = lambda w: pl.BlockSpec((tm, w), lambda i: (jnp.minimum(i, n_tiles - 1), 0))
    prev = lambda w: pl.BlockSpec((tm, w), lambda i: (jnp.maximum(i - 1, 0), 0))
    bf = lambda w: jax.ShapeDtypeStruct((m, w), BF16)
    wd = params[2]
    return pl.pallas_call(
        functools.partial(_ffn1_inproj_kernel, tiles_per_seq=t_len // tm),
        grid=(n_tiles + 1,),
        in_specs=[cur(d)] + [_resident(p.shape) for p in params],
        out_specs=[cur(d), prev(SG_WIDTH), prev(SG_WIDTH)] + [prev(DN_WIDTH)] * 5 + [prev(LANES)],
        out_shape=[jax.ShapeDtypeStruct((m, d), F32), bf(SG_WIDTH), bf(SG_WIDTH)] + [bf(DN_WIDTH)] * 5
                  + [jax.ShapeDtypeStruct((m, LANES), F32)],
        scratch_shapes=[pltpu.VMEM((tm, d), BF16), pltpu.VMEM((tm, d), BF16), pltpu.VMEM((tm, wd.shape[0]), BF16),
                        pltpu.VMEM((tm + CONV_PAD, 3 * DN_WIDTH), F32),
                        pltpu.VMEM((tm, 3 * SG_WIDTH + LANES), F32)],
        compiler_params=pltpu.CompilerParams(dimension_semantics=("arbitrary",), vmem_limit_bytes=VMEM_LIMIT),
        name="ffn1_inproj",
    )(x2d, *params)


def _outproj_ffn2(x1, mix, wout, g2, wgu, wd, gf, tm, final_norm):
    m, d = x1.shape
    tok = lambda w: pl.BlockSpec((tm, w), lambda i: (i, 0))
    return pl.pallas_call(
        functools.partial(_outproj_ffn2_kernel, final_norm=final_norm),
        grid=(m // tm,),
        in_specs=[tok(d), tok(mix.shape[1]), _resident(wout.shape), _resident(g2.shape),
                  _resident(wgu.shape), _resident(wd.shape), _resident(gf.shape)],
        out_specs=tok(d),
        out_shape=jax.ShapeDtypeStruct((m, d), F32),
        scratch_shapes=[pltpu.VMEM((tm, d), BF16), pltpu.VMEM((tm, wd.shape[0]), BF16)],
        compiler_params=pltpu.CompilerParams(dimension_semantics=("arbitrary",), vmem_limit_bytes=VMEM_LIMIT),
        name="outproj_ffn2",
    )(x1, mix, wout, g2, wgu, wd, gf)


def _mixer(acts, params, bsz, t_len, tt):
    m = acts[0].shape[0]
    nt = t_len // tt
    n_units = (tt // CHUNK) * DN_HEADS
    tok = lambda w: pl.BlockSpec((tt, w), lambda b, t: (b * nt + t, 0))
    mat = lambda dt: pltpu.VMEM((n_units, CHUNK, CHUNK), dt)
    return pl.pallas_call(
        functools.partial(_mixer_kernel, tt=tt),
        grid=(bsz, nt),
        in_specs=[tok(a.shape[1]) for a in acts] + [_resident(p.shape) for p in params],
        out_specs=tok(SG_WIDTH + DN_WIDTH),
        out_shape=jax.ShapeDtypeStruct((m, SG_WIDTH + DN_WIDTH), BF16),
        scratch_shapes=[pltpu.VMEM((DN_HEADS, DN_HEAD_DIM, DN_HEAD_DIM), F32),
                        mat(BF16), mat(BF16), mat(BF16), mat(BF16), mat(BF16), mat(F32), mat(BF16)],
        compiler_params=pltpu.CompilerParams(dimension_semantics=("arbitrary", "arbitrary"),
                                             vmem_limit_bytes=VMEM_LIMIT),
        name="mixer",
    )(*acts, *params)


def _pad_lanes(v, offset):
    return jnp.zeros((1, LANES), F32).at[0, offset:offset + v.shape[0]].set(v.astype(F32))


def kernel(x, ffn1_norm, ffn1_w_gate, ffn1_w_up, ffn1_w_down, mix_norm, w_in, conv_w, a_log, dt_bias, dn_norm,
           sg_ln_g, sg_ln_b, sg_w, sg_b, w_out, ffn2_norm, ffn2_w_gate, ffn2_w_up, ffn2_w_down, final_norm):
    bsz, t_len, d = x.shape
    depth = ffn1_norm.shape[0]
    m = bsz * t_len
    tm = math.gcd(TOKEN_TILE, t_len)
    tt = math.gcd(TIME_TILE, t_len)
    assert t_len % CHUNK == 0 and SG_CHUNK == CHUNK and sg_w.shape[-1] == SG_CHUNK
    n_main = 2 * SG_WIDTH + 4 * DN_WIDTH
    assert w_in.shape[-1] == n_main + 2 * DN_HEADS

    consts = _mask_pack()
    constsb = jnp.concatenate([-consts[:N_LEVELS], consts[N_LEVELS:]]).astype(BF16)
    causal = consts[IDX_INCL]
    row = lambda v: v.astype(F32).reshape(1, -1)
    xf = x.reshape(m, d)
    for l in range(depth):
        wsm = jnp.zeros((d, LANES), F32).at[:, :2 * DN_HEADS].set(w_in[l][:, n_main:]).astype(BF16)
        wc = sg_w[l] * causal
        wpair = jnp.concatenate([wc[0::2], wc[1::2]], axis=-1).astype(BF16)
        sgbias = jnp.repeat(sg_b[l].T.astype(F32), LANES // 2, axis=1)
        outs = _ffn1_inproj(
            xf, (row(ffn1_norm[l]), _prep_gate_up(ffn1_w_gate[l], ffn1_w_up[l]), ffn1_w_down[l].astype(BF16),
                 row(mix_norm[l]), w_in[l][:, :n_main].astype(BF16), wsm, conv_w[l].astype(F32),
                 _pad_lanes(a_log[l], DN_HEADS), _pad_lanes(dt_bias[l], DN_HEADS),
                 row(sg_ln_g[l]), row(sg_ln_b[l])), tm, t_len)
        x1, acts = outs[0], outs[1:]
        mix = _mixer(acts, (row(dn_norm[l]), wpair, sgbias, consts, constsb), bsz, t_len, tt)
        last = l == depth - 1
        xf = _outproj_ffn2(
            x1, mix, w_out[l].astype(BF16), row(ffn2_norm[l]),
            _prep_gate_up(ffn2_w_gate[l], ffn2_w_up[l]), ffn2_w_down[l].astype(BF16),
            row(final_norm) if last else row(ffn2_norm[l]), tm, last)
    return xf.reshape(bsz, t_len, d)
```

```python
import functools
import math

import jax
import jax.numpy as jnp
from jax import lax
from jax.experimental import pallas as pl
from jax.experimental.pallas import tpu as pltpu

F32 = jnp.float32
BF16 = jnp.bfloat16
EPS = 1e-6

LANES = 128
BF16_ROWS = 16
SG_GROUPS = 8
SG_CHUNK = 128
SG_WIDTH = SG_GROUPS * (LANES // 2)
DN_HEADS = 4
DN_HEAD_DIM = 128
DN_WIDTH = DN_HEADS * DN_HEAD_DIM
CONV_K = 4
CHUNK = 128
CONV_PAD = 8
FF_CHUNK = 256
TOKEN_TILE = 512
OUT_TILE = 1024
ROW_BLOCK = 64
TIME_TILE = 128
MIX_UNITS = 32
VMEM_LIMIT = 58 * 1024 * 1024

N_LEVELS = 7
IDX_STRICT = N_LEVELS
IDX_INCL = N_LEVELS + 1
IDX_EYE = N_LEVELS + 2
N_CONST = N_LEVELS + 3


def _mm(a, b):
    return jnp.dot(a, b, preferred_element_type=F32)


def _rms(x, g):
    return x * lax.rsqrt(jnp.mean(x * x, axis=-1, keepdims=True) + EPS) * g


def _silu(x):
    return x * jax.nn.sigmoid(x)


def _gelu(x):
    return jax.nn.gelu(x, approximate=True)


def _swiglu_chunk(h, wgu_ref, a_ref, c):
    cols = slice(c * FF_CHUNK, (c + 1) * FF_CHUNK)
    a_ref[:, cols] = (_silu(_mm(h, wgu_ref[0, :, cols])) * _mm(h, wgu_ref[1, :, cols])).astype(BF16)


def _swiglu(h_ref, wgu_ref, wd_ref, a_ref):
    h = h_ref[...]
    for c in range(wgu_ref.shape[2] // FF_CHUNK):
        _swiglu_chunk(h, wgu_ref, a_ref, c)
    return _mm(a_ref[...], wd_ref[...])


def _ffn1_inproj_kernel(x_ref, g1_ref, wgu_ref, wd_ref, gm_ref, win_ref, wsm_ref, convw_ref, alog_ref, dtb_ref,
                        lng_ref, lnb_ref,
                        x1_ref, u_ref, vln_ref, q_ref, k_ref, kb_ref, vb_ref, zs_ref, bg_ref,
                        h_ref, h2_ref, a_ref, cbuf_ref, st_ref, *, tiles_per_seq):
    tm = x_ref.shape[0]
    step = pl.program_id(0)

    @pl.when(step == 0)
    def _():
        h2_ref[...] = jnp.zeros_like(h2_ref)
        cbuf_ref[0:CONV_PAD, :] = jnp.zeros((CONV_PAD, cbuf_ref.shape[1]), F32)

    seq_start = (step - 1) % tiles_per_seq == 0
    cbuf_ref[0:CONV_PAD, :] = jnp.where(seq_start, 0.0, cbuf_ref[0:CONV_PAD, :])

    x = x_ref[...]
    h_ref[...] = _rms(x, g1_ref[...]).astype(BF16)
    h1 = h_ref[...]
    h2 = h2_ref[...]
    row_blocks = [slice(rb * ROW_BLOCK, (rb + 1) * ROW_BLOCK) for rb in range(tm // ROW_BLOCK)]
    lane = lax.broadcasted_iota(jnp.int32, (1, LANES), 1)
    scale = DN_HEAD_DIM ** -0.5

    def gates():
        st_ref[:, 3 * SG_WIDTH:] = _mm(h2, wsm_ref[...])
        for rows in row_blocks:
            sm = st_ref[rows, 3 * SG_WIDTH:]
            xs = sm + dtb_ref[...]
            softplus = jnp.maximum(xs, 0.0) + jnp.log(1.0 + jnp.exp(-jnp.abs(xs)))
            bg_ref[rows, :] = jnp.where(lane < DN_HEADS, jax.nn.sigmoid(sm), -jnp.exp(alog_ref[...]) * softplus)

    def conv_silu(rb, cols):
        r0 = rb * ROW_BLOCK
        xin = cbuf_ref[r0:r0 + CONV_PAD + ROW_BLOCK, cols]
        acc = convw_ref[CONV_K - 1:CONV_K, cols] * xin
        for s in range(1, CONV_K):
            acc = acc + convw_ref[CONV_K - 1 - s:CONV_K - s, cols] * pltpu.roll(xin, s, 0)
        return _silu(acc[CONV_PAD:])

    def dn_part(part):
        src = 2 * SG_WIDTH + part * DN_WIDTH
        cbuf_ref[CONV_PAD:CONV_PAD + tm, part * DN_WIDTH:(part + 1) * DN_WIDTH] = _mm(
            h2, win_ref[:, src:src + DN_WIDTH])
        for rb, rows in enumerate(row_blocks):
            for hd in range(DN_HEADS):
                hc = slice(hd * DN_HEAD_DIM, (hd + 1) * DN_HEAD_DIM)
                y = conv_silu(rb, slice(part * DN_WIDTH + hd * DN_HEAD_DIM, part * DN_WIDTH + (hd + 1) * DN_HEAD_DIM))
                if part == 0:
                    q_ref[rows, hc] = (y * (lax.rsqrt(jnp.sum(y * y, axis=-1, keepdims=True) + EPS) * scale)
                                       ).astype(BF16)
                elif part == 1:
                    k = y * lax.rsqrt(jnp.sum(y * y, axis=-1, keepdims=True) + EPS)
                    k_ref[rows, hc] = k.astype(BF16)
                    kb_ref[rows, hc] = (k * bg_ref[rows, hd:hd + 1]).astype(BF16)
                else:
                    vb_ref[rows, hc] = (y * bg_ref[rows, hd:hd + 1]).astype(BF16)

    def sg_u():
        st_ref[:, 0:SG_WIDTH] = _mm(h2, win_ref[:, 0:SG_WIDTH])
        for rows in row_blocks:
            u_ref[rows, :] = _gelu(st_ref[rows, 0:SG_WIDTH]).astype(BF16)

    def sg_v():
        st_ref[:, SG_WIDTH:2 * SG_WIDTH] = _mm(h2, win_ref[:, SG_WIDTH:2 * SG_WIDTH])
        for rows in row_blocks:
            v = _gelu(st_ref[rows, SG_WIDTH:2 * SG_WIDTH])
            vc = v - jnp.mean(v, axis=-1, keepdims=True)
            var = jnp.mean(vc * vc, axis=-1, keepdims=True)
            vln_ref[rows, :] = (vc * lax.rsqrt(var + EPS) * lng_ref[...] + lnb_ref[...]).astype(BF16)

    def dn_z():
        src = 2 * SG_WIDTH + 3 * DN_WIDTH
        st_ref[:, 2 * SG_WIDTH:3 * SG_WIDTH] = _mm(h2, win_ref[:, src:src + DN_WIDTH])
        for rows in row_blocks:
            zs_ref[rows, :] = _silu(st_ref[rows, 2 * SG_WIDTH:3 * SG_WIDTH]).astype(BF16)

    tasks = [gates, functools.partial(dn_part, 0), functools.partial(dn_part, 1), functools.partial(dn_part, 2),
             sg_u, sg_v, dn_z]
    n_ff = wgu_ref.shape[2] // FF_CHUNK
    for t, task in enumerate(tasks):
        task()
        for c in range(n_ff):
            if c * len(tasks) // n_ff == t:
                _swiglu_chunk(h1, wgu_ref, a_ref, c)
    cbuf_ref[0:CONV_PAD, :] = cbuf_ref[tm:tm + CONV_PAD, :]

    x1 = x + 0.5 * _mm(a_ref[...], wd_ref[...])
    x1_ref[...] = x1
    h2_ref[...] = _rms(x1, gm_ref[...]).astype(BF16)


def _outproj_ffn2_kernel(x1_ref, mix_ref, wout_ref, g2_ref, wgu_ref, wd_ref, gf_ref,
                         out_ref, h_ref, a_ref, *, final_norm):
    x2 = x1_ref[...] + _mm(mix_ref[...], wout_ref[...])
    h_ref[...] = _rms(x2, g2_ref[...]).astype(BF16)
    y = x2 + 0.5 * _swiglu(h_ref, wgu_ref, wd_ref, a_ref)
    out_ref[...] = _rms(y, gf_ref[...]) if final_norm else y


def _mixer_kernel(u_ref, vln_ref, q_ref, k_ref, kb_ref, vb_ref, zs_ref, bg_ref, dnn_ref,
                  wpair_ref, sgbias_ref, const_ref, constb_ref, mix_ref,
                  state_ref, lb_ref, tb_ref, zb_ref, kt_ref, qk_ref, uwy_ref, w_ref):
    bt, tt = bg_ref.shape[0], bg_ref.shape[1]
    n_chunks = tt // CHUNK
    units = [(bi, n, hd) for bi in range(bt) for n in range(n_chunks) for hd in range(DN_HEADS)]

    @pl.when(pl.program_id(1) == 0)
    def _():
        state_ref[...] = jnp.zeros_like(state_ref)

    lane = lax.broadcasted_iota(jnp.int32, (1, LANES), 1)
    lo = lane < (LANES // 2)
    zero = jnp.zeros((), BF16)
    for bi in range(bt):
        for n in range(tt // SG_CHUNK):
            rows = slice(n * SG_CHUNK, (n + 1) * SG_CHUNK)
            for p in range(SG_GROUPS // 2):
                cols = slice(p * LANES, (p + 1) * LANES)
                vp = vln_ref[bi, rows, cols]
                rhs = jnp.concatenate([jnp.where(lo, vp, zero), jnp.where(lo, zero, vp)], axis=0)
                vs = _mm(wpair_ref[p], rhs) + sgbias_ref[:, cols]
                mix_ref[bi, rows, cols] = (u_ref[bi, rows, cols].astype(F32) * vs).astype(BF16)

    strict = const_ref[IDX_STRICT]
    incl = const_ref[IDX_INCL]

    gcs, gcts = {}, {}
    row_id = lax.broadcasted_iota(jnp.int32, (CHUNK, LANES), 0)
    for bi in range(bt):
        for n in range(n_chunks):
            gc_all = bg_ref[bi, n * CHUNK:(n + 1) * CHUNK, :]
            shift = 1
            while shift < CHUNK:
                gc_all = gc_all + jnp.where(row_id >= shift, pltpu.roll(gc_all, shift, 0), 0.0)
                shift *= 2
            gcs[bi, n] = gc_all
            gcts[bi, n] = gc_all.T

    def block_of(ref, bi, n, hd):
        return ref[bi, n * CHUNK:(n + 1) * CHUNK, hd * DN_HEAD_DIM:(hd + 1) * DN_HEAD_DIM]

    def gate_col(bi, n, hd):
        return gcs[bi, n][:, DN_HEADS + hd:DN_HEADS + hd + 1]

    for ui, (bi, n, hd) in enumerate(units):
        gc = gate_col(bi, n, hd)
        gc_row = gcts[bi, n][DN_HEADS + hd:DN_HEADS + hd + 1, :]
        decay = jnp.exp(jnp.minimum(gc - gc_row, 0.0))
        kt = block_of(k_ref, bi, n, hd).T
        kt_ref[ui] = kt
        a = _mm(jnp.concatenate([block_of(kb_ref, bi, n, hd), block_of(q_ref, bi, n, hd)], axis=0), kt)
        lb = (a[:CHUNK] * decay * strict).astype(BF16)
        lb_ref[ui] = lb
        qk_ref[ui] = (a[CHUNK:] * decay * incl).astype(BF16)
        tb_ref[ui] = constb_ref[IDX_EYE] + lb * constb_ref[0]

    for lvl in range(1, N_LEVELS):
        blk = 1 << lvl
        neg_mask = constb_ref[lvl]
        if blk < BF16_ROWS:
            for ui in range(len(units)):
                zb_ref[ui] = _mm(tb_ref[ui], lb_ref[ui] * neg_mask).astype(BF16)
            for ui in range(len(units)):
                tb_ref[ui] = tb_ref[ui] + _mm(zb_ref[ui], tb_ref[ui]).astype(BF16)
        else:
            starts = range(blk, CHUNK, 2 * blk)
            half = CHUNK // 2
            for ui in range(len(units)):
                t_low = jnp.concatenate([tb_ref[ui, r:r + blk, :] for r in starts], axis=0)
                zb_ref[ui, 0:half, :] = _mm(t_low, lb_ref[ui] * neg_mask).astype(BF16)
            for ui in range(len(units)):
                new = _mm(zb_ref[ui, 0:half, :], tb_ref[ui]).astype(BF16)
                for i, r in enumerate(starts):
                    tb_ref[ui, r:r + blk, :] = tb_ref[ui, r:r + blk, :] + new[i * blk:(i + 1) * blk]

    for ui, (bi, n, hd) in enumerate(units):
        t_mat = tb_ref[ui]
        e_row = jnp.exp(gcts[bi, n][DN_HEADS + hd:DN_HEADS + hd + 1, :])
        uwy_ref[ui] = _mm(t_mat, block_of(vb_ref, bi, n, hd))
        w_ref[ui] = _mm((t_mat.astype(F32) * e_row).astype(BF16), block_of(kb_ref, bi, n, hd)).astype(BF16)

    for n in range(n_chunks):
        chunk_units = [(ui, bi, hd) for ui, (bi, cn, hd) in enumerate(units) if cn == n]
        for ui, bi, hd in chunk_units:
            gc = gate_col(bi, n, hd)
            s = state_ref[bi * DN_HEADS + hd]
            p = _mm(jnp.concatenate([w_ref[ui], block_of(q_ref, bi, n, hd)], axis=0), s.astype(BF16))
            v_new = uwy_ref[ui] - p[:CHUNK]
            zb_ref[ui] = v_new.astype(BF16)
            lb_ref[ui] = (v_new * jnp.exp(gc[CHUNK - 1:CHUNK] - gc)).astype(BF16)
            uwy_ref[ui] = p[CHUNK:] * jnp.exp(gc)
        for ui, bi, hd in chunk_units:
            gc_last = gate_col(bi, n, hd)[CHUNK - 1:CHUNK]
            o = uwy_ref[ui] + _mm(qk_ref[ui], zb_ref[ui])
            si = bi * DN_HEADS + hd
            state_ref[si] = state_ref[si] * jnp.exp(gc_last) + _mm(kt_ref[ui], lb_ref[ui])
            o = o * lax.rsqrt(jnp.mean(o * o, axis=-1, keepdims=True) + EPS) * dnn_ref[...]
            mix_ref[bi, n * CHUNK:(n + 1) * CHUNK, SG_WIDTH + hd * DN_HEAD_DIM:SG_WIDTH + (hd + 1) * DN_HEAD_DIM] = (
                o * block_of(zs_ref, bi, n, hd).astype(F32)).astype(BF16)


def _resident(shape):
    nd = len(shape)
    return pl.BlockSpec(shape, lambda *_: (0,) * nd, pipeline_mode=pl.Buffered(1))


def _prep_gate_up(w_gate, w_up):
    return jnp.stack([w_gate, w_up]).astype(BF16)


def _mask_pack():
    i = jnp.arange(CHUNK)[:, None]
    j = jnp.arange(CHUNK)[None, :]
    rows = []
    for lvl in range(N_LEVELS):
        b = 1 << lvl
        rows.append((i // (2 * b) == j // (2 * b)) & ((i // b) % 2 == 1) & ((j // b) % 2 == 0))
    rows.append(i > j)
    rows.append(i >= j)
    rows.append(i == j)
    return jnp.stack(rows).astype(F32)


def _ffn1_inproj(x2d, params, tm, t_len):
    m, d = x2d.shape
    n_tiles = m // tm
    cur = lambda w: pl.BlockSpec((tm, w), lambda i: (jnp.minimum(i, n_tiles - 1), 0))
    prev = lambda w: pl.BlockSpec((tm, w), lambda i: (jnp.maximum(i - 1, 0), 0))
    bf = lambda w: jax.ShapeDtypeStruct((m, w), BF16)
    wd = params[2]
    return pl.pallas_call(
        functools.partial(_ffn1_inproj_kernel, tiles_per_seq=t_len // tm),
        grid=(n_tiles + 1,),
        in_specs=[cur(d)] + [_resident(p.shape) for p in params],
        out_specs=[cur(d), prev(SG_WIDTH), prev(SG_WIDTH)] + [prev(DN_WIDTH)] * 5 + [prev(LANES)],
        out_shape=[jax.ShapeDtypeStruct((m, d), F32), bf(SG_WIDTH), bf(SG_WIDTH)] + [bf(DN_WIDTH)] * 5
                  + [jax.ShapeDtypeStruct((m, LANES), F32)],
        scratch_shapes=[pltpu.VMEM((tm, d), BF16), pltpu.VMEM((tm, d), BF16), pltpu.VMEM((tm, wd.shape[0]), BF16),
                        pltpu.VMEM((tm + CONV_PAD, 3 * DN_WIDTH), F32),
                        pltpu.VMEM((tm, 3 * SG_WIDTH + LANES), F32)],
        compiler_params=pltpu.CompilerParams(dimension_semantics=("arbitrary",), vmem_limit_bytes=VMEM_LIMIT),
        name="ffn1_inproj",
    )(x2d, *params)


def _outproj_ffn2(x1, mix, wout, g2, wgu, wd, gf, tm, final_norm):
    m, d = x1.shape
    tok = lambda w: pl.BlockSpec((tm, w), lambda i: (i, 0))
    return pl.pallas_call(
        functools.partial(_outproj_ffn2_kernel, final_norm=final_norm),
        grid=(m // tm,),
        in_specs=[tok(d), tok(mix.shape[1]), _resident(wout.shape), _resident(g2.shape),
                  _resident(wgu.shape), _resident(wd.shape), _resident(gf.shape)],
        out_specs=tok(d),
        out_shape=jax.ShapeDtypeStruct((m, d), F32),
        scratch_shapes=[pltpu.VMEM((tm, d), BF16), pltpu.VMEM((tm, wd.shape[0]), BF16)],
        compiler_params=pltpu.CompilerParams(dimension_semantics=("arbitrary",), vmem_limit_bytes=VMEM_LIMIT),
        name="outproj_ffn2",
    )(x1, mix, wout, g2, wgu, wd, gf)


def _mixer(acts, params, bsz, t_len, bt, tt):
    acts = [a.reshape(bsz, t_len, a.shape[1]) for a in acts]
    n_units = bt * (tt // CHUNK) * DN_HEADS
    tok = lambda w: pl.BlockSpec((bt, tt, w), lambda b, t: (b, t, 0))
    mat = lambda dt: pltpu.VMEM((n_units, CHUNK, CHUNK), dt)
    mix = pl.pallas_call(
        _mixer_kernel,
        grid=(bsz // bt, t_len // tt),
        in_specs=[tok(a.shape[2]) for a in acts] + [_resident(p.shape) for p in params],
        out_specs=tok(SG_WIDTH + DN_WIDTH),
        out_shape=jax.ShapeDtypeStruct((bsz, t_len, SG_WIDTH + DN_WIDTH), BF16),
        scratch_shapes=[pltpu.VMEM((bt * DN_HEADS, DN_HEAD_DIM, DN_HEAD_DIM), F32),
                        mat(BF16), mat(BF16), mat(BF16), mat(BF16), mat(BF16), mat(F32), mat(BF16)],
        compiler_params=pltpu.CompilerParams(dimension_semantics=("arbitrary", "arbitrary"),
                                             vmem_limit_bytes=VMEM_LIMIT),
        name="mixer",
    )(*acts, *params)
    return mix.reshape(bsz * t_len, SG_WIDTH + DN_WIDTH)


def _pad_lanes(v, offset):
    return jnp.zeros((1, LANES), F32).at[0, offset:offset + v.shape[0]].set(v.astype(F32))


def kernel(x, ffn1_norm, ffn1_w_gate, ffn1_w_up, ffn1_w_down, mix_norm, w_in, conv_w, a_log, dt_bias, dn_norm,
           sg_ln_g, sg_ln_b, sg_w, sg_b, w_out, ffn2_norm, ffn2_w_gate, ffn2_w_up, ffn2_w_down, final_norm):
    bsz, t_len, d = x.shape
    depth = ffn1_norm.shape[0]
    m = bsz * t_len
    tm = math.gcd(TOKEN_TILE, t_len)
    tt = math.gcd(TIME_TILE, t_len)
    bt = math.gcd(MIX_UNITS // (DN_HEADS * (tt // CHUNK)), bsz)
    assert t_len % CHUNK == 0 and SG_CHUNK == CHUNK and sg_w.shape[-1] == SG_CHUNK
    n_main = 2 * SG_WIDTH + 4 * DN_WIDTH
    assert w_in.shape[-1] == n_main + 2 * DN_HEADS

    consts = _mask_pack()
    constsb = jnp.concatenate([-consts[:N_LEVELS], consts[N_LEVELS:]]).astype(BF16)
    causal = consts[IDX_INCL]
    row = lambda v: v.astype(F32).reshape(1, -1)
    xf = x.reshape(m, d)
    for l in range(depth):
        wsm = jnp.zeros((d, LANES), F32).at[:, :2 * DN_HEADS].set(w_in[l][:, n_main:]).astype(BF16)
        wc = sg_w[l] * causal
        wpair = jnp.concatenate([wc[0::2], wc[1::2]], axis=-1).astype(BF16)
        sgbias = jnp.repeat(sg_b[l].T.astype(F32), LANES // 2, axis=1)
        outs = _ffn1_inproj(
            xf, (row(ffn1_norm[l]), _prep_gate_up(ffn1_w_gate[l], ffn1_w_up[l]), ffn1_w_down[l].astype(BF16),
                 row(mix_norm[l]), w_in[l][:, :n_main].astype(BF16), wsm, conv_w[l].astype(F32),
                 _pad_lanes(a_log[l], DN_HEADS), _pad_lanes(dt_bias[l], DN_HEADS),
                 row(sg_ln_g[l]), row(sg_ln_b[l])), tm, t_len)
        x1, acts = outs[0], outs[1:]
        mix = _mixer(acts, (row(dn_norm[l]), wpair, sgbias, consts, constsb), bsz, t_len, bt, tt)
        last = l == depth - 1
        xf = _outproj_ffn2(
            x1, mix, w_out[l].astype(BF16), row(ffn2_norm[l]),
            _prep_gate_up(ffn2_w_gate[l], ffn2_w_up[l]), ffn2_w_down[l].astype(BF16),
            row(final_norm) if last else row(ffn2_norm[l]), math.gcd(OUT_TILE, m), last)
    return xf.reshape(bsz, t_len, d)
```

```python
import functools
import math

import jax
import jax.numpy as jnp
from jax import lax
from jax.experimental import pallas as pl
from jax.experimental.pallas import tpu as pltpu

F32 = jnp.float32
BF16 = jnp.bfloat16
EPS = 1e-6

LANES = 128
BF16_ROWS = 16
SG_GROUPS = 8
SG_CHUNK = 128
SG_WIDTH = SG_GROUPS * (LANES // 2)
DN_HEADS = 4
DN_HEAD_DIM = 128
DN_WIDTH = DN_HEADS * DN_HEAD_DIM
CONV_K = 4
CHUNK = 128
CONV_PAD = 8
FF_CHUNK = 256
FFN_TILE = 1024
PROJ_TILE = 512
ROW_BLOCK = 64
TIME_TILE = 128
MIX_UNITS = 32
VMEM_LIMIT = 58 * 1024 * 1024

N_LEVELS = 7
IDX_STRICT = N_LEVELS
IDX_INCL = N_LEVELS + 1
IDX_EYE = N_LEVELS + 2
N_CONST = N_LEVELS + 3


def _mm(a, b):
    return jnp.dot(a, b, preferred_element_type=F32)


def _rms(x, g):
    return x * lax.rsqrt(jnp.mean(x * x, axis=-1, keepdims=True) + EPS) * g


def _silu(x):
    return x * jax.nn.sigmoid(x)


def _gelu(x):
    return jax.nn.gelu(x, approximate=True)


def _swiglu(h_ref, wgu_ref, wd_ref, a_ref):
    h = h_ref[...]
    for c in range(wgu_ref.shape[2] // FF_CHUNK):
        cols = slice(c * FF_CHUNK, (c + 1) * FF_CHUNK)
        a_ref[:, cols] = (_silu(_mm(h, wgu_ref[0, :, cols])) * _mm(h, wgu_ref[1, :, cols])).astype(BF16)
    return _mm(a_ref[...], wd_ref[...])


def _ffn1_kernel(x_ref, g1_ref, wgu_ref, wd_ref, gm_ref, x1_ref, h2_ref, h_ref, a_ref):
    x = x_ref[...]
    h_ref[...] = _rms(x, g1_ref[...]).astype(BF16)
    x1 = x + 0.5 * _swiglu(h_ref, wgu_ref, wd_ref, a_ref)
    x1_ref[...] = x1
    h2_ref[...] = _rms(x1, gm_ref[...]).astype(BF16)


def _outproj_ffn2_kernel(x1_ref, mix_ref, wout_ref, g2_ref, wgu_ref, wd_ref, gf_ref,
                         out_ref, h_ref, a_ref, *, final_norm):
    x2 = x1_ref[...] + _mm(mix_ref[...], wout_ref[...])
    h_ref[...] = _rms(x2, g2_ref[...]).astype(BF16)
    y = x2 + 0.5 * _swiglu(h_ref, wgu_ref, wd_ref, a_ref)
    out_ref[...] = _rms(y, gf_ref[...]) if final_norm else y


def _inproj_kernel(h2_ref, win_ref, wsm_ref, convw_ref, alog_ref, dtb_ref, lng_ref, lnb_ref,
                   u_ref, vln_ref, q_ref, k_ref, kb_ref, vb_ref, zs_ref, bg_ref,
                   cbuf_ref, st_ref, *, tiles_per_seq):
    tm = h2_ref.shape[0]
    step = pl.program_id(0)

    @pl.when(step == 0)
    def _():
        cbuf_ref[...] = jnp.zeros_like(cbuf_ref)
        st_ref[...] = jnp.zeros_like(st_ref)

    h2 = h2_ref[...]
    starts_seq = step % tiles_per_seq == 0
    row_blocks = [slice(rb * ROW_BLOCK, (rb + 1) * ROW_BLOCK) for rb in range(tm // ROW_BLOCK)]
    lane = lax.broadcasted_iota(jnp.int32, (1, LANES), 1)
    scale = DN_HEAD_DIM ** -0.5

    def gates():
        for rows in row_blocks:
            sm = st_ref[rows, 3 * SG_WIDTH:]
            xs = sm + dtb_ref[...]
            softplus = jnp.maximum(xs, 0.0) + jnp.log(1.0 + jnp.exp(-jnp.abs(xs)))
            bg_ref[rows, :] = jnp.where(lane < DN_HEADS, jax.nn.sigmoid(sm), -jnp.exp(alog_ref[...]) * softplus)
        st_ref[:, 3 * SG_WIDTH:] = _mm(h2, wsm_ref[...])

    def conv_silu(rb, cols):
        r0 = rb * ROW_BLOCK
        xin = cbuf_ref[r0:r0 + CONV_PAD + ROW_BLOCK, cols]
        acc = convw_ref[CONV_K - 1:CONV_K, cols] * xin
        for s in range(1, CONV_K):
            acc = acc + convw_ref[CONV_K - 1 - s:CONV_K - s, cols] * pltpu.roll(xin, s, 0)
        return _silu(acc[CONV_PAD:])

    def dn_part(part):
        pcols = slice(part * DN_WIDTH, (part + 1) * DN_WIDTH)
        for rb, rows in enumerate(row_blocks):
            for hd in range(DN_HEADS):
                hc = slice(hd * DN_HEAD_DIM, (hd + 1) * DN_HEAD_DIM)
                y = conv_silu(rb, slice(part * DN_WIDTH + hd * DN_HEAD_DIM, part * DN_WIDTH + (hd + 1) * DN_HEAD_DIM))
                if part == 0:
                    q_ref[rows, hc] = (y * (lax.rsqrt(jnp.sum(y * y, axis=-1, keepdims=True) + EPS) * scale)
                                       ).astype(BF16)
                elif part == 1:
                    k = y * lax.rsqrt(jnp.sum(y * y, axis=-1, keepdims=True) + EPS)
                    k_ref[rows, hc] = k.astype(BF16)
                    kb_ref[rows, hc] = (k * bg_ref[rows, hd:hd + 1]).astype(BF16)
                else:
                    vb_ref[rows, hc] = (y * bg_ref[rows, hd:hd + 1]).astype(BF16)
        cbuf_ref[0:CONV_PAD, pcols] = jnp.where(starts_seq, 0.0, cbuf_ref[tm:tm + CONV_PAD, pcols])
        src = 2 * SG_WIDTH + part * DN_WIDTH
        cbuf_ref[CONV_PAD:CONV_PAD + tm, pcols] = _mm(h2, win_ref[:, src:src + DN_WIDTH])

    def sg_u():
        for rows in row_blocks:
            u_ref[rows, :] = _gelu(st_ref[rows, 0:SG_WIDTH]).astype(BF16)
        st_ref[:, 0:SG_WIDTH] = _mm(h2, win_ref[:, 0:SG_WIDTH])

    def sg_v():
        for rows in row_blocks:
            v = _gelu(st_ref[rows, SG_WIDTH:2 * SG_WIDTH])
            vc = v - jnp.mean(v, axis=-1, keepdims=True)
            var = jnp.mean(vc * vc, axis=-1, keepdims=True)
            vln_ref[rows, :] = (vc * lax.rsqrt(var + EPS) * lng_ref[...] + lnb_ref[...]).astype(BF16)
        st_ref[:, SG_WIDTH:2 * SG_WIDTH] = _mm(h2, win_ref[:, SG_WIDTH:2 * SG_WIDTH])

    def dn_z():
        for rows in row_blocks:
            zs_ref[rows, :] = _silu(st_ref[rows, 2 * SG_WIDTH:3 * SG_WIDTH]).astype(BF16)
        src = 2 * SG_WIDTH + 3 * DN_WIDTH
        st_ref[:, 2 * SG_WIDTH:3 * SG_WIDTH] = _mm(h2, win_ref[:, src:src + DN_WIDTH])

    gates()
    for part in range(3):
        dn_part(part)
    sg_u()
    sg_v()
    dn_z()


def _mixer_kernel(u_ref, vln_ref, q_ref, k_ref, kb_ref, vb_ref, zs_ref, bg_ref, dnn_ref,
                  wpair_ref, sgbias_ref, const_ref, constb_ref, mix_ref,
                  state_ref, lb_ref, tb_ref, zb_ref, kt_ref, qk_ref, uwy_ref, w_ref):
    bt, tt = bg_ref.shape[0], bg_ref.shape[1]
    n_chunks = tt // CHUNK
    units = [(bi, n, hd) for bi in range(bt) for n in range(n_chunks) for hd in range(DN_HEADS)]

    @pl.when(pl.program_id(1) == 0)
    def _():
        state_ref[...] = jnp.zeros_like(state_ref)

    lane = lax.broadcasted_iota(jnp.int32, (1, LANES), 1)
    lo = lane < (LANES // 2)
    zero = jnp.zeros((), BF16)
    for bi in range(bt):
        for n in range(tt // SG_CHUNK):
            rows = slice(n * SG_CHUNK, (n + 1) * SG_CHUNK)
            for p in range(SG_GROUPS // 2):
                cols = slice(p * LANES, (p + 1) * LANES)
                vp = vln_ref[bi, rows, cols]
                rhs = jnp.concatenate([jnp.where(lo, vp, zero), jnp.where(lo, zero, vp)], axis=0)
                vs = _mm(wpair_ref[p], rhs) + sgbias_ref[:, cols]
                mix_ref[bi, rows, cols] = (u_ref[bi, rows, cols].astype(F32) * vs).astype(BF16)

    strict = const_ref[IDX_STRICT]
    incl = const_ref[IDX_INCL]

    gcs, gcts = {}, {}
    row_id = lax.broadcasted_iota(jnp.int32, (CHUNK, LANES), 0)
    for bi in range(bt):
        for n in range(n_chunks):
            gc_all = bg_ref[bi, n * CHUNK:(n + 1) * CHUNK, :]
            shift = 1
            while shift < CHUNK:
                gc_all = gc_all + jnp.where(row_id >= shift, pltpu.roll(gc_all, shift, 0), 0.0)
                shift *= 2
            gcs[bi, n] = gc_all
            gcts[bi, n] = gc_all.T

    def block_of(ref, bi, n, hd):
        return ref[bi, n * CHUNK:(n + 1) * CHUNK, hd * DN_HEAD_DIM:(hd + 1) * DN_HEAD_DIM]

    def gate_col(bi, n, hd):
        return gcs[bi, n][:, DN_HEADS + hd:DN_HEADS + hd + 1]

    for ui, (bi, n, hd) in enumerate(units):
        gc = gate_col(bi, n, hd)
        gc_row = gcts[bi, n][DN_HEADS + hd:DN_HEADS + hd + 1, :]
        decay = jnp.exp(jnp.minimum(gc - gc_row, 0.0))
        kt = block_of(k_ref, bi, n, hd).T
        kt_ref[ui] = kt
        a = _mm(jnp.concatenate([block_of(kb_ref, bi, n, hd), block_of(q_ref, bi, n, hd)], axis=0), kt)
        lb = (a[:CHUNK] * decay * strict).astype(BF16)
        lb_ref[ui] = lb
        qk_ref[ui] = (a[CHUNK:] * decay * incl).astype(BF16)
        tb_ref[ui] = constb_ref[IDX_EYE] + lb * constb_ref[0]

    for lvl in range(1, N_LEVELS):
        blk = 1 << lvl
        neg_mask = constb_ref[lvl]
        if blk < BF16_ROWS:
            for ui in range(len(units)):
                zb_ref[ui] = _mm(tb_ref[ui], lb_ref[ui] * neg_mask).astype(BF16)
            for ui in range(len(units)):
                tb_ref[ui] = tb_ref[ui] + _mm(zb_ref[ui], tb_ref[ui]).astype(BF16)
        else:
            starts = range(blk, CHUNK, 2 * blk)
            half = CHUNK // 2
            for ui in range(len(units)):
                t_low = jnp.concatenate([tb_ref[ui, r:r + blk, :] for r in starts], axis=0)
                zb_ref[ui, 0:half, :] = _mm(t_low, lb_ref[ui] * neg_mask).astype(BF16)
            for ui in range(len(units)):
                new = _mm(zb_ref[ui, 0:half, :], tb_ref[ui]).astype(BF16)
                for i, r in enumerate(starts):
                    tb_ref[ui, r:r + blk, :] = tb_ref[ui, r:r + blk, :] + new[i * blk:(i + 1) * blk]

    for ui, (bi, n, hd) in enumerate(units):
        t_mat = tb_ref[ui]
        e_row = jnp.exp(gcts[bi, n][DN_HEADS + hd:DN_HEADS + hd + 1, :])
        uwy_ref[ui] = _mm(t_mat, block_of(vb_ref, bi, n, hd))
        w_ref[ui] = _mm((t_mat.astype(F32) * e_row).astype(BF16), block_of(kb_ref, bi, n, hd)).astype(BF16)

    for n in range(n_chunks):
        chunk_units = [(ui, bi, hd) for ui, (bi, cn, hd) in enumerate(units) if cn == n]
        for ui, bi, hd in chunk_units:
            gc = gate_col(bi, n, hd)
            s = state_ref[bi * DN_HEADS + hd]
            p = _mm(jnp.concatenate([w_ref[ui], block_of(q_ref, bi, n, hd)], axis=0), s.astype(BF16))
            v_new = uwy_ref[ui] - p[:CHUNK]
            zb_ref[ui] = v_new.astype(BF16)
            lb_ref[ui] = (v_new * jnp.exp(gc[CHUNK - 1:CHUNK] - gc)).astype(BF16)
            uwy_ref[ui] = p[CHUNK:] * jnp.exp(gc)
        for ui, bi, hd in chunk_units:
            gc_last = gate_col(bi, n, hd)[CHUNK - 1:CHUNK]
            o = uwy_ref[ui] + _mm(qk_ref[ui], zb_ref[ui])
            si = bi * DN_HEADS + hd
            state_ref[si] = state_ref[si] * jnp.exp(gc_last) + _mm(kt_ref[ui], lb_ref[ui])
            o = o * lax.rsqrt(jnp.mean(o * o, axis=-1, keepdims=True) + EPS) * dnn_ref[...]
            mix_ref[bi, n * CHUNK:(n + 1) * CHUNK, SG_WIDTH + hd * DN_HEAD_DIM:SG_WIDTH + (hd + 1) * DN_HEAD_DIM] = (
                o * block_of(zs_ref, bi, n, hd).astype(F32)).astype(BF16)


def _resident(shape):
    nd = len(shape)
    return pl.BlockSpec(shape, lambda *_: (0,) * nd, pipeline_mode=pl.Buffered(1))


def _prep_gate_up(w_gate, w_up):
    return jnp.stack([w_gate, w_up]).astype(BF16)


def _mask_pack():
    i = jnp.arange(CHUNK)[:, None]
    j = jnp.arange(CHUNK)[None, :]
    rows = []
    for lvl in range(N_LEVELS):
        b = 1 << lvl
        rows.append((i // (2 * b) == j // (2 * b)) & ((i // b) % 2 == 1) & ((j // b) % 2 == 0))
    rows.append(i > j)
    rows.append(i >= j)
    rows.append(i == j)
    return jnp.stack(rows).astype(F32)


def _sequential(n_axes):
    return pltpu.CompilerParams(dimension_semantics=("arbitrary",) * n_axes, vmem_limit_bytes=VMEM_LIMIT)


def _ffn1(x2d, g1, wgu, wd, gm, tm):
    m, d = x2d.shape
    tok = pl.BlockSpec((tm, d), lambda i: (i, 0))
    return pl.pallas_call(
        _ffn1_kernel,
        grid=(m // tm,),
        in_specs=[tok] + [_resident(p.shape) for p in (g1, wgu, wd, gm)],
        out_specs=[tok, tok],
        out_shape=[jax.ShapeDtypeStruct((m, d), F32), jax.ShapeDtypeStruct((m, d), BF16)],
        scratch_shapes=[pltpu.VMEM((tm, d), BF16), pltpu.VMEM((tm, wd.shape[0]), BF16)],
        compiler_params=_sequential(1),
        name="ffn1",
    )(x2d, g1, wgu, wd, gm)


def _inproj(h2, params, tm, t_len):
    m, d = h2.shape
    n_tiles = m // tm
    cur = pl.BlockSpec((tm, d), lambda i: (jnp.minimum(i, n_tiles - 1), 0))
    prev = lambda w: pl.BlockSpec((tm, w), lambda i: (jnp.maximum(i - 1, 0), 0))
    bf = lambda w: jax.ShapeDtypeStruct((m, w), BF16)
    return pl.pallas_call(
        functools.partial(_inproj_kernel, tiles_per_seq=t_len // tm),
        grid=(n_tiles + 1,),
        in_specs=[cur] + [_resident(p.shape) for p in params],
        out_specs=[prev(SG_WIDTH), prev(SG_WIDTH)] + [prev(DN_WIDTH)] * 5 + [prev(LANES)],
        out_shape=[bf(SG_WIDTH), bf(SG_WIDTH)] + [bf(DN_WIDTH)] * 5 + [jax.ShapeDtypeStruct((m, LANES), F32)],
        scratch_shapes=[pltpu.VMEM((tm + CONV_PAD, 3 * DN_WIDTH), F32),
                        pltpu.VMEM((tm, 3 * SG_WIDTH + LANES), F32)],
        compiler_params=_sequential(1),
        name="inproj",
    )(h2, *params)


def _outproj_ffn2(x1, mix, wout, g2, wgu, wd, gf, tm, final_norm):
    m, d = x1.shape
    tok = lambda w: pl.BlockSpec((tm, w), lambda i: (i, 0))
    return pl.pallas_call(
        functools.partial(_outproj_ffn2_kernel, final_norm=final_norm),
        grid=(m // tm,),
        in_specs=[tok(d), tok(mix.shape[1]), _resident(wout.shape), _resident(g2.shape),
                  _resident(wgu.shape), _resident(wd.shape), _resident(gf.shape)],
        out_specs=tok(d),
        out_shape=jax.ShapeDtypeStruct((m, d), F32),
        scratch_shapes=[pltpu.VMEM((tm, d), BF16), pltpu.VMEM((tm, wd.shape[0]), BF16)],
        compiler_params=_sequential(1),
        name="outproj_ffn2",
    )(x1, mix, wout, g2, wgu, wd, gf)


def _mixer(acts, params, bsz, t_len, bt, tt):
    acts = [a.reshape(bsz, t_len, a.shape[1]) for a in acts]
    n_units = bt * (tt // CHUNK) * DN_HEADS
    tok = lambda w: pl.BlockSpec((bt, tt, w), lambda b, t: (b, t, 0))
    mat = lambda dt: pltpu.VMEM((n_units, CHUNK, CHUNK), dt)
    mix = pl.pallas_call(
        _mixer_kernel,
        grid=(bsz // bt, t_len // tt),
        in_specs=[tok(a.shape[2]) for a in acts] + [_resident(p.shape) for p in params],
        out_specs=tok(SG_WIDTH + DN_WIDTH),
        out_shape=jax.ShapeDtypeStruct((bsz, t_len, SG_WIDTH + DN_WIDTH), BF16),
        scratch_shapes=[pltpu.VMEM((bt * DN_HEADS, DN_HEAD_DIM, DN_HEAD_DIM), F32),
                        mat(BF16), mat(BF16), mat(BF16), mat(BF16), mat(BF16), mat(F32), mat(BF16)],
        compiler_params=_sequential(2),
        name="mixer",
    )(*acts, *params)
    return mix.reshape(bsz * t_len, SG_WIDTH + DN_WIDTH)


def _pad_lanes(v, offset):
    return jnp.zeros((1, LANES), F32).at[0, offset:offset + v.shape[0]].set(v.astype(F32))


def kernel(x, ffn1_norm, ffn1_w_gate, ffn1_w_up, ffn1_w_down, mix_norm, w_in, conv_w, a_log, dt_bias, dn_norm,
           sg_ln_g, sg_ln_b, sg_w, sg_b, w_out, ffn2_norm, ffn2_w_gate, ffn2_w_up, ffn2_w_down, final_norm):
    bsz, t_len, d = x.shape
    depth = ffn1_norm.shape[0]
    m = bsz * t_len
    tf = math.gcd(FFN_TILE, m)
    tp = math.gcd(PROJ_TILE, t_len)
    tt = math.gcd(TIME_TILE, t_len)
    bt = math.gcd(MIX_UNITS // (DN_HEADS * (tt // CHUNK)), bsz)
    assert t_len % CHUNK == 0 and SG_CHUNK == CHUNK and sg_w.shape[-1] == SG_CHUNK
    n_main = 2 * SG_WIDTH + 4 * DN_WIDTH
    assert w_in.shape[-1] == n_main + 2 * DN_HEADS

    consts = _mask_pack()
    constsb = jnp.concatenate([-consts[:N_LEVELS], consts[N_LEVELS:]]).astype(BF16)
    causal = consts[IDX_INCL]
    row = lambda v: v.astype(F32).reshape(1, -1)
    xf = x.reshape(m, d)
    for l in range(depth):
        wsm = jnp.zeros((d, LANES), F32).at[:, :2 * DN_HEADS].set(w_in[l][:, n_main:]).astype(BF16)
        wc = sg_w[l] * causal
        wpair = jnp.concatenate([wc[0::2], wc[1::2]], axis=-1).astype(BF16)
        sgbias = jnp.repeat(sg_b[l].T.astype(F32), LANES // 2, axis=1)
        x1, h2 = _ffn1(xf, row(ffn1_norm[l]), _prep_gate_up(ffn1_w_gate[l], ffn1_w_up[l]),
                       ffn1_w_down[l].astype(BF16), row(mix_norm[l]), tf)
        acts = _inproj(h2, (w_in[l][:, :n_main].astype(BF16), wsm, conv_w[l].astype(F32),
                            _pad_lanes(a_log[l], DN_HEADS), _pad_lanes(dt_bias[l], DN_HEADS),
                            row(sg_ln_g[l]), row(sg_ln_b[l])), tp, t_len)
        mix = _mixer(acts, (row(dn_norm[l]), wpair, sgbias, consts, constsb), bsz, t_len, bt, tt)
        last = l == depth - 1
        xf = _outproj_ffn2(
            x1, mix, w_out[l].astype(BF16), row(ffn2_norm[l]),
            _prep_gate_up(ffn2_w_gate[l], ffn2_w_up[l]), ffn2_w_down[l].astype(BF16),
            row(final_norm) if last else row(ffn2_norm[l]), tf, last)
    return xf.reshape(bsz, t_len, d)
```

```python
import functools
import math

import jax
import jax.numpy as jnp
from jax import lax
from jax.experimental import pallas as pl
from jax.experimental.pallas import tpu as pltpu

F32 = jnp.float32
BF16 = jnp.bfloat16
EPS = 1e-6

LANES = 128
BF16_ROWS = 16
SG_GROUPS = 8
SG_CHUNK = 128
SG_WIDTH = SG_GROUPS * (LANES // 2)
DN_HEADS = 4
DN_HEAD_DIM = 128
DN_WIDTH = DN_HEADS * DN_HEAD_DIM
CONV_K = 4
CHUNK = 128
CONV_PAD = 8
FF_CHUNK = 256
FFN_TILE = 1024
PROJ_TILE = 512
ROW_BLOCK = 32
TIME_TILE = 128
MIX_UNITS = 32
VMEM_LIMIT = 58 * 1024 * 1024

N_LEVELS = 7
IDX_STRICT = N_LEVELS
IDX_INCL = N_LEVELS + 1
IDX_EYE = N_LEVELS + 2
N_CONST = N_LEVELS + 3


def _mm(a, b):
    return jnp.dot(a, b, preferred_element_type=F32)


def _rms(x, g):
    return x * lax.rsqrt(jnp.mean(x * x, axis=-1, keepdims=True) + EPS) * g


def _silu(x):
    h = 0.5 * x
    return h + h * jnp.tanh(h)


def _gelu(x):
    c0 = math.sqrt(2.0 / math.pi)
    h = 0.5 * x
    return h + h * jnp.tanh(x * (c0 + (0.044715 * c0) * (x * x)))


def _swiglu(h_ref, wgu_ref, wd_ref, a_ref):
    h = h_ref[...]
    for c in range(wgu_ref.shape[2] // FF_CHUNK):
        cols = slice(c * FF_CHUNK, (c + 1) * FF_CHUNK)
        a_ref[:, cols] = (_silu(_mm(h, wgu_ref[0, :, cols])) * _mm(h, wgu_ref[1, :, cols])).astype(BF16)
    return _mm(a_ref[...], wd_ref[...])


def _ffn1_kernel(x_ref, g1_ref, wgu_ref, wd_ref, gm_ref, x1_ref, h2_ref, h_ref, a_ref):
    x = x_ref[...]
    h_ref[...] = _rms(x, g1_ref[...]).astype(BF16)
    x1 = x + 0.5 * _swiglu(h_ref, wgu_ref, wd_ref, a_ref)
    x1_ref[...] = x1
    h2_ref[...] = _rms(x1, gm_ref[...]).astype(BF16)


def _outproj_ffn2_kernel(x1_ref, mix_ref, wout_ref, g2_ref, wgu_ref, wd_ref, gf_ref,
                         out_ref, h_ref, a_ref, *, final_norm):
    x2 = x1_ref[...] + _mm(mix_ref[...], wout_ref[...])
    h_ref[...] = _rms(x2, g2_ref[...]).astype(BF16)
    y = x2 + 0.5 * _swiglu(h_ref, wgu_ref, wd_ref, a_ref)
    out_ref[...] = _rms(y, gf_ref[...]) if final_norm else y


def _inproj_kernel(h2_ref, win_ref, wsm_ref, convw_ref, alog_ref, dtb_ref, lng_ref, lnb_ref,
                   u_ref, vln_ref, q_ref, k_ref, kb_ref, vb_ref, zs_ref, bg_ref,
                   cbuf_ref, st_ref, *, tiles_per_seq):
    tm = h2_ref.shape[0]
    step = pl.program_id(0)

    @pl.when(step == 0)
    def _():
        cbuf_ref[tm:tm + CONV_PAD, :] = jnp.zeros((CONV_PAD, cbuf_ref.shape[1]), F32)

    h2 = h2_ref[...]
    starts_seq = step % tiles_per_seq == 0
    row_blocks = [slice(rb * ROW_BLOCK, (rb + 1) * ROW_BLOCK) for rb in range(tm // ROW_BLOCK)]
    lane = lax.broadcasted_iota(jnp.int32, (1, LANES), 1)
    scale = DN_HEAD_DIM ** -0.5

    def gates():
        st_ref[:, 3 * SG_WIDTH:] = _mm(h2, wsm_ref[...])
        for rows in row_blocks:
            sm = st_ref[rows, 3 * SG_WIDTH:]
            xs = sm + dtb_ref[...]
            softplus = jnp.maximum(xs, 0.0) + jnp.log(1.0 + jnp.exp(-jnp.abs(xs)))
            bg_ref[rows, :] = jnp.where(lane < DN_HEADS, jax.nn.sigmoid(sm), -jnp.exp(alog_ref[...]) * softplus)

    def conv_silu(rb, cols):
        r0 = rb * ROW_BLOCK
        xin = cbuf_ref[r0:r0 + CONV_PAD + ROW_BLOCK, cols]
        acc = convw_ref[CONV_K - 1:CONV_K, cols] * xin
        for s in range(1, CONV_K):
            acc = acc + convw_ref[CONV_K - 1 - s:CONV_K - s, cols] * pltpu.roll(xin, s, 0)
        return _silu(acc[CONV_PAD:])

    def dn_part(part):
        pcols = slice(part * DN_WIDTH, (part + 1) * DN_WIDTH)
        cbuf_ref[0:CONV_PAD, pcols] = jnp.where(starts_seq, 0.0, cbuf_ref[tm:tm + CONV_PAD, pcols])
        src = 2 * SG_WIDTH + part * DN_WIDTH
        cbuf_ref[CONV_PAD:CONV_PAD + tm, pcols] = _mm(h2, win_ref[:, src:src + DN_WIDTH])
        for rb, rows in enumerate(row_blocks):
            for hd in range(DN_HEADS):
                hc = slice(hd * DN_HEAD_DIM, (hd + 1) * DN_HEAD_DIM)
                y = conv_silu(rb, slice(part * DN_WIDTH + hd * DN_HEAD_DIM, part * DN_WIDTH + (hd + 1) * DN_HEAD_DIM))
                if part == 0:
                    q_ref[rows, hc] = (y * (lax.rsqrt(jnp.sum(y * y, axis=-1, keepdims=True) + EPS) * scale)
                                       ).astype(BF16)
                elif part == 1:
                    k = y * lax.rsqrt(jnp.sum(y * y, axis=-1, keepdims=True) + EPS)
                    k_ref[rows, hc] = k.astype(BF16)
                    kb_ref[rows, hc] = (k * bg_ref[rows, hd:hd + 1]).astype(BF16)
                else:
                    vb_ref[rows, hc] = (y * bg_ref[rows, hd:hd + 1]).astype(BF16)

    def sg_u():
        st_ref[:, 0:SG_WIDTH] = _mm(h2, win_ref[:, 0:SG_WIDTH])
        for rows in row_blocks:
            u_ref[rows, :] = _gelu(st_ref[rows, 0:SG_WIDTH]).astype(BF16)

    def sg_v():
        st_ref[:, SG_WIDTH:2 * SG_WIDTH] = _mm(h2, win_ref[:, SG_WIDTH:2 * SG_WIDTH])
        for rows in row_blocks:
            v = _gelu(st_ref[rows, SG_WIDTH:2 * SG_WIDTH])
            vc = v - jnp.mean(v, axis=-1, keepdims=True)
            var = jnp.mean(vc * vc, axis=-1, keepdims=True)
            vln_ref[rows, :] = (vc * lax.rsqrt(var + EPS) * lng_ref[...] + lnb_ref[...]).astype(BF16)

    def dn_z():
        src = 2 * SG_WIDTH + 3 * DN_WIDTH
        st_ref[:, 2 * SG_WIDTH:3 * SG_WIDTH] = _mm(h2, win_ref[:, src:src + DN_WIDTH])
        for rows in row_blocks:
            zs_ref[rows, :] = _silu(st_ref[rows, 2 * SG_WIDTH:3 * SG_WIDTH]).astype(BF16)

    gates()
    for part in range(3):
        dn_part(part)
    sg_u()
    sg_v()
    dn_z()


def _mixer_kernel(u_ref, vln_ref, q_ref, k_ref, kb_ref, vb_ref, zs_ref, bg_ref, dnn_ref,
                  wpair_ref, sgbias_ref, const_ref, constb_ref, mix_ref,
                  state_ref, lb_ref, tb_ref, zb_ref, kt_ref, qk_ref, uwy_ref, w_ref):
    bt, tt = bg_ref.shape[0], bg_ref.shape[1]
    n_chunks = tt // CHUNK
    units = [(bi, n, hd) for bi in range(bt) for n in range(n_chunks) for hd in range(DN_HEADS)]

    @pl.when(pl.program_id(1) == 0)
    def _():
        state_ref[...] = jnp.zeros_like(state_ref)

    lane = lax.broadcasted_iota(jnp.int32, (1, LANES), 1)
    lo = lane < (LANES // 2)
    zero = jnp.zeros((), BF16)
    for bi in range(bt):
        for n in range(tt // SG_CHUNK):
            rows = slice(n * SG_CHUNK, (n + 1) * SG_CHUNK)
            for p in range(SG_GROUPS // 2):
                cols = slice(p * LANES, (p + 1) * LANES)
                vp = vln_ref[bi, rows, cols]
                rhs = jnp.concatenate([jnp.where(lo, vp, zero), jnp.where(lo, zero, vp)], axis=0)
                vs = _mm(wpair_ref[p], rhs) + sgbias_ref[:, cols]
                mix_ref[bi, rows, cols] = (u_ref[bi, rows, cols].astype(F32) * vs).astype(BF16)

    strict = const_ref[IDX_STRICT]
    incl = const_ref[IDX_INCL]

    gcs, gcts = {}, {}
    row_id = lax.broadcasted_iota(jnp.int32, (CHUNK, LANES), 0)
    for bi in range(bt):
        for n in range(n_chunks):
            gc_all = bg_ref[bi, n * CHUNK:(n + 1) * CHUNK, :]
            shift = 1
            while shift < CHUNK:
                gc_all = gc_all + jnp.where(row_id >= shift, pltpu.roll(gc_all, shift, 0), 0.0)
                shift *= 2
            gcs[bi, n] = gc_all
            gcts[bi, n] = gc_all.T

    def block_of(ref, bi, n, hd):
        return ref[bi, n * CHUNK:(n + 1) * CHUNK, hd * DN_HEAD_DIM:(hd + 1) * DN_HEAD_DIM]

    def gate_col(bi, n, hd):
        return gcs[bi, n][:, DN_HEADS + hd:DN_HEADS + hd + 1]

    for ui, (bi, n, hd) in enumerate(units):
        gc = gate_col(bi, n, hd)
        gc_row = gcts[bi, n][DN_HEADS + hd:DN_HEADS + hd + 1, :]
        decay = jnp.exp(jnp.minimum(gc - gc_row, 0.0))
        kt = block_of(k_ref, bi, n, hd).T
        kt_ref[ui] = kt
        a = _mm(jnp.concatenate([block_of(kb_ref, bi, n, hd), block_of(q_ref, bi, n, hd)], axis=0), kt)
        lb = (a[:CHUNK] * decay * strict).astype(BF16)
        lb_ref[ui] = lb
        qk_ref[ui] = (a[CHUNK:] * decay * incl).astype(BF16)
        tb_ref[ui] = constb_ref[IDX_EYE] + lb * constb_ref[0]

    for lvl in range(1, N_LEVELS):
        blk = 1 << lvl
        neg_mask = constb_ref[lvl]
        if blk < BF16_ROWS:
            for ui in range(len(units)):
                zb_ref[ui] = _mm(tb_ref[ui], lb_ref[ui] * neg_mask).astype(BF16)
            for ui in range(len(units)):
                tb_ref[ui] = tb_ref[ui] + _mm(zb_ref[ui], tb_ref[ui]).astype(BF16)
        else:
            starts = range(blk, CHUNK, 2 * blk)
            half = CHUNK // 2
            for ui in range(len(units)):
                t_low = jnp.concatenate([tb_ref[ui, r:r + blk, :] for r in starts], axis=0)
                zb_ref[ui, 0:half, :] = _mm(t_low, lb_ref[ui] * neg_mask).astype(BF16)
            for ui in range(len(units)):
                new = _mm(zb_ref[ui, 0:half, :], tb_ref[ui]).astype(BF16)
                for i, r in enumerate(starts):
                    tb_ref[ui, r:r + blk, :] = tb_ref[ui, r:r + blk, :] + new[i * blk:(i + 1) * blk]

    for ui, (bi, n, hd) in enumerate(units):
        t_mat = tb_ref[ui]
        e_row = jnp.exp(gcts[bi, n][DN_HEADS + hd:DN_HEADS + hd + 1, :])
        uwy_ref[ui] = _mm(t_mat, block_of(vb_ref, bi, n, hd))
        w_ref[ui] = _mm((t_mat.astype(F32) * e_row).astype(BF16), block_of(kb_ref, bi, n, hd)).astype(BF16)

    for n in range(n_chunks):
        chunk_units = [(ui, bi, hd) for ui, (bi, cn, hd) in enumerate(units) if cn == n]
        for ui, bi, hd in chunk_units:
            gc = gate_col(bi, n, hd)
            s = state_ref[bi * DN_HEADS + hd]
            p = _mm(jnp.concatenate([w_ref[ui], block_of(q_ref, bi, n, hd)], axis=0), s.astype(BF16))
            v_new = uwy_ref[ui] - p[:CHUNK]
            zb_ref[ui] = v_new.astype(BF16)
            lb_ref[ui] = (v_new * jnp.exp(gc[CHUNK - 1:CHUNK] - gc)).astype(BF16)
            uwy_ref[ui] = p[CHUNK:] * jnp.exp(gc)
        for ui, bi, hd in chunk_units:
            gc_last = gate_col(bi, n, hd)[CHUNK - 1:CHUNK]
            o = uwy_ref[ui] + _mm(qk_ref[ui], zb_ref[ui])
            si = bi * DN_HEADS + hd
            state_ref[si] = state_ref[si] * jnp.exp(gc_last) + _mm(kt_ref[ui], lb_ref[ui])
            o = o * lax.rsqrt(jnp.mean(o * o, axis=-1, keepdims=True) + EPS) * dnn_ref[...]
            mix_ref[bi, n * CHUNK:(n + 1) * CHUNK, SG_WIDTH + hd * DN_HEAD_DIM:SG_WIDTH + (hd + 1) * DN_HEAD_DIM] = (
                o * block_of(zs_ref, bi, n, hd).astype(F32)).astype(BF16)


def _resident(shape):
    nd = len(shape)
    return pl.BlockSpec(shape, lambda *_: (0,) * nd, pipeline_mode=pl.Buffered(1))


def _prep_gate_up(w_gate, w_up):
    return jnp.stack([w_gate, w_up]).astype(BF16)


def _mask_pack():
    i = jnp.arange(CHUNK)[:, None]
    j = jnp.arange(CHUNK)[None, :]
    rows = []
    for lvl in range(N_LEVELS):
        b = 1 << lvl
        rows.append((i // (2 * b) == j // (2 * b)) & ((i // b) % 2 == 1) & ((j // b) % 2 == 0))
    rows.append(i > j)
    rows.append(i >= j)
    rows.append(i == j)
    return jnp.stack(rows).astype(F32)


def _sequential(n_axes):
    return pltpu.CompilerParams(dimension_semantics=("arbitrary",) * n_axes, vmem_limit_bytes=VMEM_LIMIT)


def _ffn1(x2d, g1, wgu, wd, gm, tm):
    m, d = x2d.shape
    tok = pl.BlockSpec((tm, d), lambda i: (i, 0))
    return pl.pallas_call(
        _ffn1_kernel,
        grid=(m // tm,),
        in_specs=[tok] + [_resident(p.shape) for p in (g1, wgu, wd, gm)],
        out_specs=[tok, tok],
        out_shape=[jax.ShapeDtypeStruct((m, d), F32), jax.ShapeDtypeStruct((m, d), BF16)],
        scratch_shapes=[pltpu.VMEM((tm, d), BF16), pltpu.VMEM((tm, wd.shape[0]), BF16)],
        compiler_params=_sequential(1),
        name="ffn1",
    )(x2d, g1, wgu, wd, gm)


def _inproj(h2, params, tm, t_len):
    m, d = h2.shape
    tok = lambda w: pl.BlockSpec((tm, w), lambda i: (i, 0))
    bf = lambda w: jax.ShapeDtypeStruct((m, w), BF16)
    return pl.pallas_call(
        functools.partial(_inproj_kernel, tiles_per_seq=t_len // tm),
        grid=(m // tm,),
        in_specs=[tok(d)] + [_resident(p.shape) for p in params],
        out_specs=[tok(SG_WIDTH), tok(SG_WIDTH)] + [tok(DN_WIDTH)] * 5 + [tok(LANES)],
        out_shape=[bf(SG_WIDTH), bf(SG_WIDTH)] + [bf(DN_WIDTH)] * 5 + [jax.ShapeDtypeStruct((m, LANES), F32)],
        scratch_shapes=[pltpu.VMEM((tm + CONV_PAD, 3 * DN_WIDTH), F32),
                        pltpu.VMEM((tm, 3 * SG_WIDTH + LANES), F32)],
        compiler_params=_sequential(1),
        name="inproj",
    )(h2, *params)


def _outproj_ffn2(x1, mix, wout, g2, wgu, wd, gf, tm, final_norm):
    m, d = x1.shape
    tok = lambda w: pl.BlockSpec((tm, w), lambda i: (i, 0))
    return pl.pallas_call(
        functools.partial(_outproj_ffn2_kernel, final_norm=final_norm),
        grid=(m // tm,),
        in_specs=[tok(d), tok(mix.shape[1]), _resident(wout.shape), _resident(g2.shape),
                  _resident(wgu.shape), _resident(wd.shape), _resident(gf.shape)],
        out_specs=tok(d),
        out_shape=jax.ShapeDtypeStruct((m, d), F32),
        scratch_shapes=[pltpu.VMEM((tm, d), BF16), pltpu.VMEM((tm, wd.shape[0]), BF16)],
        compiler_params=_sequential(1),
        name="outproj_ffn2",
    )(x1, mix, wout, g2, wgu, wd, gf)


def _mixer(acts, params, bsz, t_len, bt, tt):
    acts = [a.reshape(bsz, t_len, a.shape[1]) for a in acts]
    n_units = bt * (tt // CHUNK) * DN_HEADS
    tok = lambda w: pl.BlockSpec((bt, tt, w), lambda b, t: (b, t, 0))
    mat = lambda dt: pltpu.VMEM((n_units, CHUNK, CHUNK), dt)
    mix = pl.pallas_call(
        _mixer_kernel,
        grid=(bsz // bt, t_len // tt),
        in_specs=[tok(a.shape[2]) for a in acts] + [_resident(p.shape) for p in params],
        out_specs=tok(SG_WIDTH + DN_WIDTH),
        out_shape=jax.ShapeDtypeStruct((bsz, t_len, SG_WIDTH + DN_WIDTH), BF16),
        scratch_shapes=[pltpu.VMEM((bt * DN_HEADS, DN_HEAD_DIM, DN_HEAD_DIM), F32),
                        mat(BF16), mat(BF16), mat(BF16), mat(BF16), mat(BF16), mat(F32), mat(BF16)],
        compiler_params=_sequential(2),
        name="mixer",
    )(*acts, *params)
    return mix.reshape(bsz * t_len, SG_WIDTH + DN_WIDTH)


def _pad_lanes(v, offset):
    return jnp.zeros((1, LANES), F32).at[0, offset:offset + v.shape[0]].set(v.astype(F32))


def kernel(x, ffn1_norm, ffn1_w_gate, ffn1_w_up, ffn1_w_down, mix_norm, w_in, conv_w, a_log, dt_bias, dn_norm,
           sg_ln_g, sg_ln_b, sg_w, sg_b, w_out, ffn2_norm, ffn2_w_gate, ffn2_w_up, ffn2_w_down, final_norm):
    bsz, t_len, d = x.shape
    depth = ffn1_norm.shape[0]
    m = bsz * t_len
    tf = math.gcd(FFN_TILE, m)
    tp = math.gcd(PROJ_TILE, t_len)
    tt = math.gcd(TIME_TILE, t_len)
    bt = math.gcd(MIX_UNITS // (DN_HEADS * (tt // CHUNK)), bsz)
    assert t_len % CHUNK == 0 and SG_CHUNK == CHUNK and sg_w.shape[-1] == SG_CHUNK
    n_main = 2 * SG_WIDTH + 4 * DN_WIDTH
    assert w_in.shape[-1] == n_main + 2 * DN_HEADS

    consts = _mask_pack()
    constsb = jnp.concatenate([-consts[:N_LEVELS], consts[N_LEVELS:]]).astype(BF16)
    causal = consts[IDX_INCL]
    row = lambda v: v.astype(F32).reshape(1, -1)
    xf = x.reshape(m, d)
    for l in range(depth):
        wsm = jnp.zeros((d, LANES), F32).at[:, :2 * DN_HEADS].set(w_in[l][:, n_main:]).astype(BF16)
        wc = sg_w[l] * causal
        wpair = jnp.concatenate([wc[0::2], wc[1::2]], axis=-1).astype(BF16)
        sgbias = jnp.repeat(sg_b[l].T.astype(F32), LANES // 2, axis=1)
        x1, h2 = _ffn1(xf, row(ffn1_norm[l]), _prep_gate_up(ffn1_w_gate[l], ffn1_w_up[l]),
                       ffn1_w_down[l].astype(BF16), row(mix_norm[l]), tf)
        acts = _inproj(h2, (w_in[l][:, :n_main].astype(BF16), wsm, conv_w[l].astype(F32),
                            _pad_lanes(a_log[l], DN_HEADS), _pad_lanes(dt_bias[l], DN_HEADS),
                            row(sg_ln_g[l]), row(sg_ln_b[l])), tp, t_len)
        mix = _mixer(acts, (row(dn_norm[l]), wpair, sgbias, consts, constsb), bsz, t_len, bt, tt)
        last = l == depth - 1
        xf = _outproj_ffn2(
            x1, mix, w_out[l].astype(BF16), row(ffn2_norm[l]),
            _prep_gate_up(ffn2_w_gate[l], ffn2_w_up[l]), ffn2_w_down[l].astype(BF16),
            row(final_norm) if last else row(ffn2_norm[l]), tf, last)
    return xf.reshape(bsz, t_len, d)
```
